```python
import math
import jax, jax.numpy as jnp
from jax import lax
import numpy as np

D_MODEL = 2048
BATCH = 4
SEQ = 4096
DEPTH = 4

CHUNK = 64
N_MIXERS = 4
CONV_KERNEL = 31
LN_EPS = 1e-5
RWKV_HEAD_DIM = 64
RWKV_HEADS = D_MODEL // RWKV_HEAD_DIM
DECAY_LORA = 96
AAA_LORA = 96
GATE_LORA = 256
GN_EPS = 64e-5
SHORT_KERNEL = 3
FOX_HEAD_DIM = 128
FOX_HEADS = D_MODEL // FOX_HEAD_DIM
Q_BLOCK = 128
D_FF = 4 * D_MODEL
RMS_EPS = 1e-6

kernel_name = "hybrid_conformer_rwkv7_shortconv_fox_trunk"


def _n_layers_of(m):
    return (DEPTH - m + N_MIXERS - 1) // N_MIXERS


def rmsnorm(x, g):
    xf = x.astype(jnp.float32)
    y = xf * lax.rsqrt(jnp.mean(xf * xf, axis=-1, keepdims=True) + RMS_EPS)
    return (y * g.astype(jnp.float32)).astype(x.dtype)


def layernorm(x, g, b):
    xf = x.astype(jnp.float32)
    mu = jnp.mean(xf, axis=-1, keepdims=True)
    var = jnp.mean(jnp.square(xf - mu), axis=-1, keepdims=True)
    y = (xf - mu) * lax.rsqrt(var + LN_EPS)
    return (y * g.astype(jnp.float32) + b.astype(jnp.float32)).astype(x.dtype)


def causal_dwconv(x, w):
    K = w.shape[0]
    return lax.conv_general_dilated(
        x, w[:, None, :].astype(x.dtype), window_strides=(1,), padding=[(K - 1, 0)],
        dimension_numbers=('NWC', 'WIO', 'NWC'), feature_group_count=x.shape[-1])


def conformer_conv(h, w_in, b_in, dw, dw_b, ln_g, ln_b, w_out, b_out):
    u = h @ w_in + b_in
    val, gate = jnp.split(u, 2, axis=-1)
    u = val * jax.nn.sigmoid(gate)
    u = causal_dwconv(u, dw) + dw_b
    u = jax.nn.silu(layernorm(u, ln_g, ln_b))
    return u @ w_out + b_out


def rwkv7_time_mix(h, mu, w_rkv, w0, w1, w2, a0, a1, a2, g1, g2, k_k, k_a, r_k, gn_g, gn_b, w_o):
    B, T, D = h.shape
    H, N = RWKV_HEADS, RWKV_HEAD_DIM
    f32 = jnp.float32
    xx = jnp.pad(h, ((0, 0), (1, 0), (0, 0)))[:, :T] - h
    xr, xw, xk, xv, xa, xg = (h + xx * mu[i] for i in range(6))
    r = xr @ w_rkv[0]
    k = xk @ w_rkv[1]
    v = xv @ w_rkv[2]
    w = -jax.nn.softplus(-(w0 + jnp.tanh(xw @ w1) @ w2).astype(f32)) - 0.5
    a = jax.nn.sigmoid(a0 + (xa @ a1) @ a2)
    g = jax.nn.sigmoid(xg @ g1) @ g2
    kk = (k * k_k).astype(f32).reshape(B, T, H, N)
    kk = kk / jnp.maximum(jnp.sqrt(jnp.sum(kk * kk, axis=-1, keepdims=True)), 1e-12)
    k = k * (1 + (a - 1) * k_a)

    def heads(t):
        return t.astype(f32).reshape(B, T, H, N)

    r_h, k_h, v_h, a_h = heads(r), heads(k), heads(v), heads(a)
    decay = jnp.exp(-jnp.exp(w)).reshape(B, T, H, N)

    def to_chunks(t):
        return t.transpose(1, 0, 2, 3).reshape(T // CHUNK, CHUNK, B, H, N)

    xs = tuple(to_chunks(t) for t in (r_h, decay, k_h, v_h, -kk, kk * a_h))

    def frame_step(S, inp):
        r_t, w_t, k_t, v_t, a_t, b_t = inp
        sa = jnp.einsum('bhij,bhj->bhi', S, a_t)
        S = S * w_t[:, :, None, :] + sa[..., None] * b_t[:, :, None, :] + v_t[..., None] * k_t[:, :, None, :]
        y = jnp.einsum('bhij,bhj->bhi', S, r_t)
        return S, y

    def chunk_step(S, chunk):
        return lax.scan(frame_step, S, chunk)

    S0 = jnp.zeros((B, H, N, N), f32)
    _, y = lax.scan(chunk_step, S0, xs)
    y = y.reshape(T, B, H, N).transpose(1, 0, 2, 3)
    mean = jnp.mean(y, axis=-1, keepdims=True)
    var = jnp.mean(jnp.square(y - mean), axis=-1, keepdims=True)
    y = ((y - mean) * lax.rsqrt(var + GN_EPS)).reshape(B, T, D) * gn_g.astype(f32) + gn_b.astype(f32)
    bonus = (jnp.sum(r_h * k_h * r_k.astype(f32), axis=-1, keepdims=True) * v_h).reshape(B, T, D)
    y = (y + bonus).astype(h.dtype)
    return (y * g) @ w_o


def short_conv(h, w_in, conv_w, w_out):
    u = h @ w_in
    gate_b, gate_c, hv = jnp.split(u, 3, axis=-1)
    z = causal_dwconv(gate_c * hv, conv_w)
    return (gate_b * z) @ w_out


def fox_attention(h, w_qkvf, b_f, w_o):
    B, T, D = h.shape
    H, Dh = FOX_HEADS, FOX_HEAD_DIM
    f32 = jnp.float32
    u = h @ w_qkvf

    def heads(t):
        return t.reshape(B, T, H, Dh).transpose(0, 2, 1, 3)

    q, k, v = heads(u[..., :D]), heads(u[..., D:2 * D]), heads(u[..., 2 * D:3 * D])
    log_f = jax.nn.log_sigmoid((u[..., 3 * D:] + b_f).astype(f32))
    c = jnp.cumsum(log_f, axis=1).transpose(0, 2, 1)
    scale = 1.0 / math.sqrt(Dh)
    outs = []
    for blk in range(T // Q_BLOCK):
        q0 = blk * Q_BLOCK
        q1 = q0 + Q_BLOCK
        s = jnp.einsum('bhqd,bhkd->bhqk', q[:, :, q0:q1], k[:, :, :q1]).astype(f32) * scale
        s = s + c[:, :, q0:q1, None] - c[:, :, None, :q1]
        mask = (q0 + jnp.arange(Q_BLOCK))[:, None] >= jnp.arange(q1)[None, :]
        s = jnp.where(mask, s, -jnp.inf)
        p = jax.nn.softmax(s, axis=-1).astype(v.dtype)
        outs.append(jnp.einsum('bhqk,bhkd->bhqd', p, v[:, :, :q1]))
    o = jnp.concatenate(outs, axis=2).transpose(0, 2, 1, 3).reshape(B, T, D)
    return o @ w_o


def sq_relu_mlp(h, w1, w2):
    return jnp.square(jax.nn.relu(h @ w1)) @ w2


def setup_inputs(seed: int = 0) -> dict:
    key = jax.random.key(seed)
    ks = iter(jax.random.split(key, 48))
    D = D_MODEL
    nA, nB, nC, nD = (_n_layers_of(m) for m in range(N_MIXERS))
    H, N = RWKV_HEADS, RWKV_HEAD_DIM

    def nrm(shape, scale):
        return jax.random.normal(next(ks), shape, jnp.float32) * scale

    def gain(shape):
        return 1.0 + nrm(shape, 0.02)

    inp = {}
    inp["x"] = nrm((BATCH, SEQ, D), 1.0)
    inp["norm1_g"] = gain((DEPTH, D))
    inp["norm2_g"] = gain((DEPTH, D))
    inp["mlp_w1"] = nrm((DEPTH, D, D_FF), D ** -0.5)
    inp["mlp_w2"] = nrm((DEPTH, D_FF, D), D_FF ** -0.5)
    inp["cc_w_in"] = nrm((nA, D, 2 * D), D ** -0.5)
    inp["cc_b_in"] = nrm((nA, 2 * D), 0.02)
    inp["cc_dw"] = nrm((nA, CONV_KERNEL, D), CONV_KERNEL ** -0.5)
    inp["cc_dw_b"] = nrm((nA, D), 0.02)
    inp["cc_ln_g"] = gain((nA, D))
    inp["cc_ln_b"] = nrm((nA, D), 0.02)
    inp["cc_w_out"] = nrm((nA, D, D), D ** -0.5)
    inp["cc_b_out"] = nrm((nA, D), 0.02)
    inp["rw_mu"] = jax.random.uniform(next(ks), (nB, 6, D), jnp.float32)
    inp["rw_w_rkv"] = nrm((nB, 3, D, D), D ** -0.5)
    inp["rw_w0"] = jax.random.uniform(next(ks), (nB, D), jnp.float32, -6.0, -1.0)
    inp["rw_w1"] = nrm((nB, D, DECAY_LORA), D ** -0.5)
    inp["rw_w2"] = nrm((nB, DECAY_LORA, D), 0.5 * DECAY_LORA ** -0.5)
    inp["rw_a0"] = nrm((nB, D), 0.02)
    inp["rw_a1"] = nrm((nB, D, AAA_LORA), D ** -0.5)
    inp["rw_a2"] = nrm((nB, AAA_LORA, D), AAA_LORA ** -0.5)
    inp["rw_g1"] = nrm((nB, D, GATE_LORA), D ** -0.5)
    inp["rw_g2"] = nrm((nB, GATE_LORA, D), GATE_LORA ** -0.5)
    inp["rw_k_k"] = 0.85 + nrm((nB, D), 0.1)
    inp["rw_k_a"] = 1.0 + nrm((nB, D), 0.1)
    inp["rw_r_k"] = nrm((nB, H, N), 0.1)
    inp["rw_gn_g"] = gain((nB, D))
    inp["rw_gn_b"] = nrm((nB, D), 0.02)
    inp["rw_w_o"] = nrm((nB, D, D), D ** -0.5)
    inp["sc_w_in"] = nrm((nC, D, 3 * D), D ** -0.5)
    inp["sc_conv_w"] = nrm((nC, SHORT_KERNEL, D), SHORT_KERNEL ** -0.5)
    inp["sc_w_out"] = nrm((nC, D, D), D ** -0.5)
    w_qkv = nrm((nD, D, 3 * D), D ** -0.5)
    w_f = nrm((nD, D, FOX_HEADS), 0.1 * D ** -0.5)
    inp["fx_w_qkvf"] = jnp.concatenate([w_qkv, w_f], axis=-1)
    inp["fx_b_f"] = jax.random.uniform(next(ks), (nD, FOX_HEADS), jnp.float32, 1.0, 5.0)
    inp["fx_w_o"] = nrm((nD, D, D), D ** -0.5)
    inp["final_g"] = gain((D,))
    return inp


def reference(x, norm1_g, norm2_g, mlp_w1, mlp_w2,
              cc_w_in, cc_b_in, cc_dw, cc_dw_b, cc_ln_g, cc_ln_b, cc_w_out, cc_b_out,
              rw_mu, rw_w_rkv, rw_w0, rw_w1, rw_w2, rw_a0, rw_a1, rw_a2, rw_g1, rw_g2,
              rw_k_k, rw_k_a, rw_r_k, rw_gn_g, rw_gn_b, rw_w_o,
              sc_w_in, sc_conv_w, sc_w_out,
              fx_w_qkvf, fx_b_f, fx_w_o, final_g):
    for i in range(DEPTH):
        m, j = i % N_MIXERS, i // N_MIXERS
        h = rmsnorm(x, norm1_g[i])
        if m == 0:
            y = conformer_conv(h, cc_w_in[j], cc_b_in[j], cc_dw[j], cc_dw_b[j],
                               cc_ln_g[j], cc_ln_b[j], cc_w_out[j], cc_b_out[j])
        elif m == 1:
            y = rwkv7_time_mix(h, rw_mu[j], rw_w_rkv[j], rw_w0[j], rw_w1[j], rw_w2[j],
                               rw_a0[j], rw_a1[j], rw_a2[j], rw_g1[j], rw_g2[j],
                               rw_k_k[j], rw_k_a[j], rw_r_k[j], rw_gn_g[j], rw_gn_b[j], rw_w_o[j])
        elif m == 2:
            y = short_conv(h, sc_w_in[j], sc_conv_w[j], sc_w_out[j])
        else:
            y = fox_attention(h, fx_w_qkvf[j], fx_b_f[j], fx_w_o[j])
        x = x + y
        x = x + sq_relu_mlp(rmsnorm(x, norm2_g[i]), mlp_w1[i], mlp_w2[i])
    return rmsnorm(x, final_g)
```

```python
import functools
import math

import jax
import jax.numpy as jnp
from jax import lax
from jax.experimental import pallas as pl
from jax.experimental.pallas import tpu as pltpu

F32 = jnp.float32
BF16 = jnp.bfloat16

RMS_EPS = 1e-6
LN_EPS = 1e-5
GN_EPS = 64e-5
KK_EPS = 1e-12

LANES = 128
SUBLANES = 8
VMEM_LIMIT = 52 * 1024 * 1024

RWKV_CHUNK = 64
NEG_BIG = -1e30


def _cparams(*sem):
    return pltpu.CompilerParams(dimension_semantics=sem, vmem_limit_bytes=VMEM_LIMIT)


def _tile(n, want):
    t = min(n, want)
    while n % t:
        t -= 1
    return t


def _row(v):
    return v.reshape(1, -1).astype(F32)


def _rms(x, g):
    return x * lax.rsqrt(jnp.mean(x * x, axis=-1, keepdims=True) + RMS_EPS) * g


def _dot(a, b):
    return jnp.dot(a.astype(BF16), b.astype(BF16), preferred_element_type=F32)


def _dot_nt(a, b):
    return lax.dot_general(a.astype(BF16), b.astype(BF16), (((1,), (1,)), ((), ())),
                           preferred_element_type=F32)


def _dot_tn(a, b):
    return lax.dot_general(a.astype(BF16), b.astype(BF16), (((0,), (0,)), ((), ())),
                           preferred_element_type=F32)


def _softplus(z):
    return jnp.maximum(z, 0.0) + jnp.log1p(jnp.exp(-jnp.abs(z)))


def _sigmoid(z):
    return 1.0 / (1.0 + jnp.exp(-z))


def _tri_cumsum(tri, v):
    hi = v.astype(BF16)
    r1 = v - hi.astype(F32)
    mid = r1.astype(BF16)
    lo = (r1 - mid.astype(F32)).astype(BF16)
    dot = functools.partial(jnp.dot, preferred_element_type=F32)
    return dot(tri, hi) + dot(tri, mid) + dot(tri, lo)


def _norm_mm_kernel(*refs, n_w, n_b, n_out, epilogue):
    x_ref, g_ref = refs[:2]
    w_refs = refs[2:2 + n_w]
    b_refs = refs[2 + n_w:2 + n_w + n_b]
    o_refs = refs[2 + n_w + n_b:2 + n_w + n_b + n_out]
    xn_ref = refs[-1]

    @pl.when(pl.program_id(1) == 0)
    def _():
        xn_ref[...] = _rms(x_ref[...], g_ref[...]).astype(BF16)

    xn = xn_ref[...]
    ys = [jnp.dot(xn, w[...], preferred_element_type=F32) for w in w_refs]
    if n_b:
        ys = [y + b[...] for y, b in zip(ys, b_refs)]
    for o, val in zip(o_refs, epilogue(*ys)):
        o[...] = val.astype(o.dtype)


def _norm_mm(x, g, w, n_groups, epilogue, out_dtypes, biases=None, tm=512, tn=512, name=None):
    M, D = x.shape
    N = w.shape[1] // n_groups
    tm, tn = _tile(M, tm), _tile(N, tn)
    nb = N // tn
    w_specs = [pl.BlockSpec((D, tn), functools.partial(lambda i, j, q: (0, j + q * nb), q=q))
               for q in range(n_groups)]
    args = [x, _row(g)] + [w] * n_groups
    specs = [pl.BlockSpec((tm, D), lambda i, j: (i, 0)), pl.BlockSpec((1, D), lambda i, j: (0, 0))] + w_specs
    n_b = 0
    if biases is not None:
        n_b = n_groups
        args += [_row(biases)] * n_groups
        specs += [pl.BlockSpec((1, tn), functools.partial(lambda i, j, q: (0, j + q * nb), q=q))
                  for q in range(n_groups)]
    outs = pl.pallas_call(
        functools.partial(_norm_mm_kernel, n_w=n_groups, n_b=n_b, n_out=len(out_dtypes), epilogue=epilogue),
        grid=(M // tm, nb),
        in_specs=specs,
        out_specs=[pl.BlockSpec((tm, tn), lambda i, j: (i, j)) for _ in out_dtypes],
        out_shape=[jax.ShapeDtypeStruct((M, N), dt) for dt in out_dtypes],
        scratch_shapes=[pltpu.VMEM((tm, D), BF16)],
        compiler_params=_cparams("parallel", "arbitrary"),
        name=name,
    )(*args)
    return outs


def _mm_res_kernel(*refs, n_in, has_b, prologue):
    in_refs = refs[:n_in]
    w_ref = refs[n_in]
    b_ref = refs[n_in + 1] if has_b else None
    res_ref = refs[n_in + 1 + has_b]
    o_ref = refs[n_in + 2 + has_b]
    act_ref = refs[-1]

    @pl.when(pl.program_id(1) == 0)
    def _():
        act_ref[...] = prologue(*[r[...] for r in in_refs]).astype(BF16)

    y = jnp.dot(act_ref[...], w_ref[...], preferred_element_type=F32)
    if has_b:
        y = y + b_ref[...]
    o_ref[...] = res_ref[...] + y


def _mm_res(ins, w, res, prologue, bias=None, tm=512, tn=512, name=None):
    M, K = ins[0].shape
    N = w.shape[1]
    tm, tn = _tile(M, tm), _tile(N, tn)
    args = list(ins) + [w]
    specs = [pl.BlockSpec((tm, K), lambda i, j: (i, 0)) for _ in ins] + [pl.BlockSpec((K, tn), lambda i, j: (0, j))]
    if bias is not None:
        args.append(_row(bias))
        specs.append(pl.BlockSpec((1, tn), lambda i, j: (0, j)))
    args.append(res)
    specs.append(pl.BlockSpec((tm, tn), lambda i, j: (i, j)))
    return pl.pallas_call(
        functools.partial(_mm_res_kernel, n_in=len(ins), has_b=bias is not None, prologue=prologue),
        grid=(M // tm, N // tn),
        in_specs=specs,
        out_specs=pl.BlockSpec((tm, tn), lambda i, j: (i, j)),
        out_shape=jax.ShapeDtypeStruct((M, N), F32),
        scratch_shapes=[pltpu.VMEM((tm, K), BF16)],
        compiler_params=_cparams("parallel", "arbitrary"),
        name=name,
    )(*args)


def _mlp_kernel(*refs, final):
    x_ref, g_ref, w1_ref, w2_ref = refs[:4]
    fg_ref = refs[4] if final else None
    o_ref, xn_ref, acc_ref = refs[-3:]
    f = pl.program_id(1)

    @pl.when(f == 0)
    def _():
        xn_ref[...] = _rms(x_ref[...], g_ref[...]).astype(BF16)
        acc_ref[...] = jnp.zeros_like(acc_ref)

    h = jnp.dot(xn_ref[...], w1_ref[...], preferred_element_type=F32)
    h = jnp.square(jnp.maximum(h, 0.0)).astype(BF16)
    acc_ref[...] += jnp.dot(h, w2_ref[...], preferred_element_type=F32)

    @pl.when(f == pl.num_programs(1) - 1)
    def _():
        y = x_ref[...] + acc_ref[...]
        if final:
            y = _rms(y, fg_ref[...])
        o_ref[...] = y


def _mlp(x, g, w1, w2, final_g=None, tm=512, tf=512):
    M, D = x.shape
    F = w1.shape[1]
    tm, tf = _tile(M, tm), _tile(F, tf)
    final = final_g is not None
    args = [x, _row(g), w1, w2]
    specs = [pl.BlockSpec((tm, D), lambda i, f: (i, 0)), pl.BlockSpec((1, D), lambda i, f: (0, 0)),
             pl.BlockSpec((D, tf), lambda i, f: (0, f)), pl.BlockSpec((tf, D), lambda i, f: (f, 0))]
    if final:
        args.append(_row(final_g))
        specs.append(pl.BlockSpec((1, D), lambda i, f: (0, 0)))
    return pl.pallas_call(
        functools.partial(_mlp_kernel, final=final),
        grid=(M // tm, F // tf),
        in_specs=specs,
        out_specs=pl.BlockSpec((tm, D), lambda i, f: (i, 0)),
        out_shape=jax.ShapeDtypeStruct((M, D), F32),
        scratch_shapes=[pltpu.VMEM((tm, D), BF16), pltpu.VMEM((tm, D), F32)],
        compiler_params=_cparams("parallel", "arbitrary"),
        name="mlp_final" if final else "mlp",
    )(*args)


def _conv_mm_kernel(*refs, n_par, has_b, halo, make_act):
    cur_refs = refs[:2]
    halo_ref = refs[2]
    par_refs = refs[3:3 + n_par]
    w_ref = refs[3 + n_par]
    b_ref = refs[4 + n_par] if has_b else None
    res_ref = refs[4 + n_par + has_b]
    o_ref = refs[5 + n_par + has_b]
    ubuf_ref, act_ref = refs[-2:]
    tt = act_ref.shape[0]

    @pl.when(pl.program_id(2) == 0)
    def _():
        first = pl.program_id(1) == 0
        ubuf_ref[0:halo, :] = jnp.where(first, 0.0, halo_ref[0])
        ubuf_ref[halo:halo + tt, :] = cur_refs[0][0]
        make_act(ubuf_ref, cur_refs[1], [p[...] for p in par_refs], act_ref)

    y = jnp.dot(act_ref[...], w_ref[...], preferred_element_type=F32)
    if has_b:
        y = y + b_ref[...]
    o_ref[0] = res_ref[0] + y


def _conv_mm(u, side, params, w, res, make_act, halo, bias=None, tt=256, tn=512, name=None):
    B, T, D = u.shape
    N = w.shape[1]
    tt, tn = _tile(T, tt), _tile(N, tn)
    hb = tt // halo
    args = [u, side, u] + list(params) + [w]
    specs = [pl.BlockSpec((1, tt, D), lambda b, t, j: (b, t, 0)),
             pl.BlockSpec((1, tt, D), lambda b, t, j: (b, t, 0)),
             pl.BlockSpec((1, halo, D), lambda b, t, j: (b, jnp.maximum(t * hb - 1, 0), 0))]
    specs += [pl.BlockSpec(p.shape, lambda b, t, j: (0, 0)) for p in params]
    specs.append(pl.BlockSpec((D, tn), lambda b, t, j: (0, j)))
    if bias is not None:
        args.append(_row(bias))
        specs.append(pl.BlockSpec((1, tn), lambda b, t, j: (0, j)))
    args.append(res)
    specs.append(pl.BlockSpec((1, tt, tn), lambda b, t, j: (b, t, j)))
    return pl.pallas_call(
        functools.partial(_conv_mm_kernel, n_par=len(params), has_b=bias is not None, halo=halo,
                          make_act=make_act),
        grid=(B, T // tt, N // tn),
        in_specs=specs,
        out_specs=pl.BlockSpec((1, tt, tn), lambda b, t, j: (b, t, j)),
        out_shape=jax.ShapeDtypeStruct((B, T, N), F32),
        scratch_shapes=[pltpu.VMEM((halo + tt, D), F32), pltpu.VMEM((tt, D), BF16)],
        compiler_params=_cparams("parallel", "parallel", "arbitrary"),
        name=name,
    )(*args)


def _conformer_act(ubuf_ref, _side_ref, params, act_ref, *, taps, halo, rb, cc):
    dw, dw_b, ln_g, ln_b = params
    tt, D = act_ref.shape
    off = halo - (taps - 1)
    for r0 in range(0, tt, rb):
        pieces = []
        for c0 in range(0, D, cc):
            acc = jnp.zeros((rb, cc), F32)
            for k in range(taps):
                acc = acc + dw[k:k + 1, c0:c0 + cc] * ubuf_ref[off + k + r0:off + k + r0 + rb, c0:c0 + cc]
            pieces.append(acc + dw_b[:, c0:c0 + cc])
        mu = sum(jnp.sum(p, axis=-1, keepdims=True) for p in pieces) / D
        var = sum(jnp.sum(jnp.square(p - mu), axis=-1, keepdims=True) for p in pieces) / D
        inv = lax.rsqrt(var + LN_EPS)
        for c0, p in zip(range(0, D, cc), pieces):
            z = (p - mu) * inv * ln_g[:, c0:c0 + cc] + ln_b[:, c0:c0 + cc]
            act_ref[r0:r0 + rb, c0:c0 + cc] = (z * _sigmoid(z)).astype(BF16)


def _shortconv_act(ubuf_ref, gate_ref, params, act_ref, *, taps, halo, rb):
    (cw,) = params
    tt, _ = act_ref.shape
    off = halo - (taps - 1)
    for r0 in range(0, tt, rb):
        acc = cw[0:1, :] * ubuf_ref[off + r0:off + r0 + rb, :]
        for k in range(1, taps):
            acc = acc + cw[k:k + 1, :] * ubuf_ref[off + k + r0:off + k + r0 + rb, :]
        act_ref[r0:r0 + rb, :] = (gate_ref[0, r0:r0 + rb, :] * acc).astype(BF16)


def _rwkv_prep_kernel(x_ref, halo_ref, g_ref, mu_ref, w0_ref, w1_ref, w2_ref, a0_ref, a1_ref, a2_ref,
                      g1_ref, g2_ref, xrkv_ref, wp_ref, a_ref, gate_ref):
    g = g_ref[...]
    h = _rms(x_ref[0], g)
    prev_last = _rms(halo_ref[0, SUBLANES - 1:SUBLANES, :], g)
    prev_last = jnp.where(pl.program_id(1) == 0, 0.0, prev_last)
    row = lax.broadcasted_iota(jnp.int32, h.shape, 0)
    hprev = jnp.where(row == 0, prev_last, pltpu.roll(h, 1, 0))
    xx = hprev - h
    mu = mu_ref[...]
    xrkv_ref[0, 0] = (h + xx * mu[0:1]).astype(BF16)
    xrkv_ref[1, 0] = (h + xx * mu[2:3]).astype(BF16)
    xrkv_ref[2, 0] = (h + xx * mu[3:4]).astype(BF16)
    xw = h + xx * mu[1:2]
    xa = h + xx * mu[4:5]
    xg = h + xx * mu[5:6]
    wp_ref[0] = w0_ref[...] + _dot(jnp.tanh(_dot(xw, w1_ref[...])), w2_ref[...])
    a_ref[0] = _sigmoid(a0_ref[...] + _dot(_dot(xa, a1_ref[...]), a2_ref[...]))
    gate_ref[0] = _dot(_sigmoid(_dot(xg, g1_ref[...])), g2_ref[...])


def _rwkv_prep(x, g, mu, w0, w1, w2, a0, a1, a2, g1, g2, tt=256):
    B, T, D = x.shape
    tt = _tile(T, tt)
    hb = tt // SUBLANES
    full = lambda a: pl.BlockSpec(a.shape, lambda b, t: (0,) * a.ndim)
    small = [_row(g), mu, _row(w0), w1, w2, _row(a0), a1, a2, g1, g2]
    tile = pl.BlockSpec((1, tt, D), lambda b, t: (b, t, 0))
    return pl.pallas_call(
        _rwkv_prep_kernel,
        grid=(B, T // tt),
        in_specs=[tile, pl.BlockSpec((1, SUBLANES, D), lambda b, t: (b, jnp.maximum(t * hb - 1, 0), 0))]
        + [full(a) for a in small],
        out_specs=[pl.BlockSpec((3, 1, tt, D), lambda b, t: (0, b, t, 0)), tile, tile, tile],
        out_shape=[jax.ShapeDtypeStruct((3, B, T, D), BF16)] + [jax.ShapeDtypeStruct((B, T, D), F32)] * 3,
        compiler_params=_cparams("parallel", "arbitrary"),
        name="rwkv_prep",
    )(x, x, *small)


def _bmm_kernel(x_ref, w_ref, o_ref):
    o_ref[0] = jnp.dot(x_ref[0], w_ref[0], preferred_element_type=F32)


def _bmm(x, w, tm=512, tn=1024):
    G, M, K = x.shape
    N = w.shape[2]
    tm, tn = _tile(M, tm), _tile(N, tn)
    return pl.pallas_call(
        _bmm_kernel,
        grid=(G, M // tm, N // tn),
        in_specs=[pl.BlockSpec((1, tm, K), lambda q, i, j: (q, i, 0)),
                  pl.BlockSpec((1, K, tn), lambda q, i, j: (q, 0, j))],
        out_specs=pl.BlockSpec((1, tm, tn), lambda q, i, j: (q, i, j)),
        out_shape=jax.ShapeDtypeStruct((G, M, N), F32),
        compiler_params=_cparams("parallel", "parallel", "arbitrary"),
        name="rwkv_rkv",
    )(x, w)


def _rwkv_scan_kernel(r_ref, k_ref, v_ref, wp_ref, a_ref, kk_ref, ka_ref, rk_ref, gg_ref, gb_ref,
                      o_ref, s_ref, *, hd, chunk):
    L = chunk
    tb, hw = r_ref.shape[1], r_ref.shape[2]

    @pl.when(pl.program_id(2) == 0)
    def _():
        s_ref[...] = jnp.zeros_like(s_ref)

    rowi = lax.broadcasted_iota(jnp.int32, (L, L), 0)
    coli = lax.broadcasted_iota(jnp.int32, (L, L), 1)
    tri_bf = (rowi >= coli).astype(BF16)
    strict = rowi > coli
    row2 = lax.broadcasted_iota(jnp.int32, (L, 2 * L), 0)
    col2 = lax.broadcasted_iota(jnp.int32, (L, 2 * L), 1)
    col2 = jnp.where(col2 >= L, col2 - L, col2)
    strict2 = row2 > col2
    incl2 = row2 >= col2
    eye = (rowi == coli).astype(F32)

    kk_p, ka_p, rk_p, gg_p, gb_p = kk_ref[...], ka_ref[...], rk_ref[...], gg_ref[...], gb_ref[...]

    for c in range(tb // L):
        rows = slice(c * L, (c + 1) * L)
        r_all, k_all, v_all = r_ref[0, rows, :], k_ref[0, rows, :], v_ref[0, rows, :]
        a_all = a_ref[0, rows, :]
        lw = -jnp.exp(-_softplus(-wp_ref[0, rows, :]) - 0.5)
        cum = _tri_cumsum(tri_bf, lw)
        cum_end = cum[L - 1:L, :]
        w_in, w_ex, w_inv = jnp.exp(cum), jnp.exp(cum - lw), jnp.exp(-cum)
        w_end, w_last = jnp.exp(cum_end - cum), jnp.exp(cum_end)
        kkv_all = k_all * kk_p
        k2_all = k_all * (1.0 + (a_all - 1.0) * ka_p)
        rt_all = r_all * w_in
        kt_all = k2_all * w_inv
        ke_all = k2_all * w_end
        bonus_all = r_all * k2_all * rk_p

        for h in range(hw // hd):
            sl = slice(h * hd, (h + 1) * hd)
            kkv = kkv_all[:, sl]
            kk = kkv / jnp.maximum(jnp.sqrt(jnp.sum(kkv * kkv, axis=-1, keepdims=True)), KK_EPS)
            b = kk * a_all[:, sl]
            v = v_all[:, sl]
            at = -kk * w_ex[:, sl]
            bt = b * w_inv[:, sl]
            be = b * w_end[:, sl]
            rt = rt_all[:, sl]

            aa = _dot_nt(jnp.concatenate([at, rt], axis=0), jnp.concatenate([kt_all[:, sl], bt], axis=0))
            top = jnp.where(strict2, aa[:L], 0.0)
            bot = jnp.where(incl2, aa[L:], 0.0)
            a_ak, x = top[:, :L], top[:, L:]

            t_inv = eye + x
            for _ in range(int(math.log2(L)) - 1):
                x = _dot(x, x)
                t_inv = t_inv + _dot(t_inv, x)
            t_at = _dot(t_inv, at)
            q1 = _dot(t_inv, _dot(a_ak, v))

            s0 = s_ref[h]
            rs = _dot_nt(jnp.concatenate([t_at, rt], axis=0), s0)
            u = rs[:L] + q1
            vu = jnp.concatenate([v, u], axis=0)
            y = rs[L:] + _dot(bot, vu)
            s_ref[h] = s0 * w_last[:, sl] + _dot_tn(vu, jnp.concatenate([ke_all[:, sl], be], axis=0))

            mean = jnp.mean(y, axis=-1, keepdims=True)
            var = jnp.mean(jnp.square(y - mean), axis=-1, keepdims=True)
            yn = (y - mean) * lax.rsqrt(var + GN_EPS) * gg_p[:, sl] + gb_p[:, sl]
            o_ref[0, rows, sl] = yn + jnp.sum(bonus_all[:, sl], axis=-1, keepdims=True) * v


def _rwkv_scan(r, k, v, wp, a, k_k, k_a, r_k, gn_g, gn_b, hd, tb=256, hw=LANES):
    B, T, D = r.shape
    tb = _tile(T, tb)
    tile = pl.BlockSpec((1, tb, hw), lambda b, h, t: (b, t, h))
    par = pl.BlockSpec((1, hw), lambda b, h, t: (0, h))
    return pl.pallas_call(
        functools.partial(_rwkv_scan_kernel, hd=hd, chunk=_tile(tb, RWKV_CHUNK)),
        grid=(B, D // hw, T // tb),
        in_specs=[tile] * 5 + [par] * 5,
        out_specs=tile,
        out_shape=jax.ShapeDtypeStruct((B, T, D), F32),
        scratch_shapes=[pltpu.VMEM((hw // hd, hd, hd), F32)],
        compiler_params=_cparams("parallel", "parallel", "arbitrary"),
        name="rwkv_scan",
    )(r, k, v, wp, a, _row(k_k), _row(k_a), _row(r_k), _row(gn_g), _row(gn_b))


def _fox_gate_kernel(x_ref, g_ref, wf_ref, bf_ref, c_ref, carry_ref):
    @pl.when(pl.program_id(1) == 0)
    def _():
        carry_ref[...] = jnp.zeros_like(carry_ref)

    tt = x_ref.shape[1]
    z = _dot(_rms(x_ref[0], g_ref[...]), wf_ref[...]) + bf_ref[...]
    log_f = -_softplus(-z)
    tri = (lax.broadcasted_iota(jnp.int32, (tt, tt), 0) >= lax.broadcasted_iota(jnp.int32, (tt, tt), 1))
    c = carry_ref[...] + _tri_cumsum(tri.astype(BF16), log_f)
    c_ref[0] = c
    carry_ref[...] = c[tt - 1:tt, :]


def _fox_gate(x, g, wf, bf, tt=256):
    B, T, D = x.shape
    tt = _tile(T, tt)
    return pl.pallas_call(
        _fox_gate_kernel,
        grid=(B, T // tt),
        in_specs=[pl.BlockSpec((1, tt, D), lambda b, t: (b, t, 0)), pl.BlockSpec((1, D), lambda b, t: (0, 0)),
                  pl.BlockSpec((D, LANES), lambda b, t: (0, 0)), pl.BlockSpec((1, LANES), lambda b, t: (0, 0))],
        out_specs=pl.BlockSpec((1, tt, LANES), lambda b, t: (b, t, 0)),
        out_shape=jax.ShapeDtypeStruct((B, T, LANES), F32),
        scratch_shapes=[pltpu.VMEM((1, LANES), F32)],
        compiler_params=_cparams("parallel", "arbitrary"),
        name="fox_gate",
    )(x, _row(g), wf, bf)


def _fox_attn_kernel(q_ref, k_ref, v_ref, cq_ref, ck_ref, o_ref, *, tq, tk):
    h, qi = pl.program_id(1), pl.program_id(2)
    q = q_ref[0]
    lane = lax.broadcasted_iota(jnp.int32, cq_ref.shape[1:], 1)
    cq = jnp.sum(jnp.where(lane == h, cq_ref[0], 0.0), axis=-1, keepdims=True)
    nd = tq // tk

    def step(kb, carry, diag):
        m, l, acc = carry
        ks = k_ref[0, pl.ds(pl.multiple_of(kb * tk, tk), tk), :]
        vs = v_ref[0, pl.ds(pl.multiple_of(kb * tk, tk), tk), :]
        s = lax.dot_general(q, ks, (((1,), (1,)), ((), ())), preferred_element_type=F32)
        s = s + (cq - ck_ref[0, 0, pl.ds(kb, 1), :])
        if diag is not None:
            rowi = lax.broadcasted_iota(jnp.int32, (tq, tk), 0)
            coli = lax.broadcasted_iota(jnp.int32, (tq, tk), 1) + diag * tk
            s = jnp.where(rowi >= coli, s, NEG_BIG)
        m_new = jnp.maximum(m, jnp.max(s, axis=-1, keepdims=True))
        alpha = jnp.exp(m - m_new)
        p = jnp.exp(s - m_new)
        l = alpha * l + jnp.sum(p, axis=-1, keepdims=True)
        acc = alpha * acc + jnp.dot(p.astype(BF16), vs, preferred_element_type=F32)
        return m_new, l, acc

    init = (jnp.full((tq, 1), NEG_BIG, F32), jnp.zeros((tq, 1), F32), jnp.zeros((tq, q.shape[1]), F32))
    carry = lax.fori_loop(0, qi * nd, lambda kb, cr: step(kb, cr, None), init)
    for d in range(nd):
        carry = step(qi * nd + d, carry, d)
    _, l, acc = carry
    o_ref[0] = (acc / l).astype(o_ref.dtype)


def _fox_attn(q, k, v, c, ct, n_heads, tq=512, tk=512):
    B, T, D = q.shape
    dh = D // n_heads
    return pl.pallas_call(
        functools.partial(_fox_attn_kernel, tq=tq, tk=tk),
        grid=(B, n_heads, T // tq),
        in_specs=[pl.BlockSpec((1, tq, dh), lambda b, h, i: (b, i, h)),
                  pl.BlockSpec((1, T, dh), lambda b, h, i: (b, 0, h)),
                  pl.BlockSpec((1, T, dh), lambda b, h, i: (b, 0, h)),
                  pl.BlockSpec((1, tq, LANES), lambda b, h, i: (b, i, 0)),
                  pl.BlockSpec((1, 1, T // tk, tk), lambda b, h, i: (b, h, 0, 0))],
        out_specs=pl.BlockSpec((1, tq, dh), lambda b, h, i: (b, i, h)),
        out_shape=jax.ShapeDtypeStruct((B, T, D), BF16),
        compiler_params=_cparams("parallel", "parallel", "arbitrary"),
        name="fox_attn",
    )(q, k, v, c, ct)


def _conformer(x, g, w_in, b_in, dw, dw_b, ln_g, ln_b, w_out, b_out):
    B, T, D = x.shape
    (u,) = _norm_mm(x.reshape(B * T, D), g, w_in.astype(BF16), 2,
                    lambda val, gate: (val * _sigmoid(gate),), [F32], biases=b_in, name="cc_in")
    taps = dw.shape[0]
    halo = -(-(taps - 1) // SUBLANES) * SUBLANES
    make_act = functools.partial(_conformer_act, taps=taps, halo=halo, rb=32, cc=_tile(D, 512))
    u = u.reshape(B, T, D)
    return _conv_mm(u, u, [dw, _row(dw_b), _row(ln_g), _row(ln_b)], w_out.astype(BF16), x, make_act, halo,
                    bias=b_out, name="cc_out")


def _short_conv(x, g, w_in, conv_w, w_out):
    B, T, D = x.shape
    gate_b, p = _norm_mm(x.reshape(B * T, D), g, w_in.astype(BF16), 3,
                         lambda gb, gc, hv: (gb, gc * hv), [F32, F32], name="sc_in")
    taps = conv_w.shape[0]
    halo = -(-(taps - 1) // SUBLANES) * SUBLANES
    make_act = functools.partial(_shortconv_act, taps=taps, halo=halo, rb=64)
    return _conv_mm(p.reshape(B, T, D), gate_b.reshape(B, T, D), [conv_w], w_out.astype(BF16), x, make_act,
                    halo, name="sc_out")


def _pad_cols(w, n):
    return jnp.pad(w, ((0, 0), (0, n - w.shape[1])))


def _pad_rows(w, n):
    return jnp.pad(w, ((0, n - w.shape[0]), (0, 0)))


def _rwkv7(x, g, mu, w_rkv, w0, w1, w2, a0, a1, a2, g1, g2, k_k, k_a, r_k, gn_g, gn_b, w_o):
    B, T, D = x.shape
    hd = r_k.shape[-1]
    rank = lambda w: -(-w.shape[1] // LANES) * LANES
    xrkv, wp, a, gate = _rwkv_prep(
        x, g, mu, w0,
        _pad_cols(w1, rank(w1)).astype(BF16), _pad_rows(w2, rank(w1)).astype(BF16), a0,
        _pad_cols(a1, rank(a1)).astype(BF16), _pad_rows(a2, rank(a1)).astype(BF16),
        _pad_cols(g1, rank(g1)).astype(BF16), _pad_rows(g2, rank(g1)).astype(BF16))
    rkv = _bmm(xrkv.reshape(3, B * T, D), w_rkv.astype(BF16)).reshape(3, B, T, D)
    y = _rwkv_scan(rkv[0], rkv[1], rkv[2], wp, a, k_k, k_a, r_k.reshape(-1), gn_g, gn_b, hd)
    out = _mm_res([y.reshape(B * T, D), gate.reshape(B * T, D)], w_o.astype(BF16), x.reshape(B * T, D),
                  lambda yy, gg: yy * gg, name="rwkv_out")
    return out.reshape(B, T, D)


def _fox(x, g, w_qkvf, b_f, w_o, tk=512):
    B, T, D = x.shape
    H = b_f.shape[0]
    scale = 1.0 / math.sqrt(D // H)
    q, k, v = _norm_mm(x.reshape(B * T, D), g, w_qkvf[:, :3 * D].astype(BF16), 3,
                       lambda q_, k_, v_: (q_ * scale, k_, v_), [BF16] * 3, name="fx_qkv")
    c = _fox_gate(x, g, _pad_cols(w_qkvf[:, 3 * D:], LANES).astype(BF16), _pad_cols(_row(b_f), LANES))
    tk = _tile(T, tk)
    ct = jnp.transpose(c[:, :, :H], (0, 2, 1)).reshape(B, H, T // tk, tk)
    o = _fox_attn(q.reshape(B, T, D), k.reshape(B, T, D), v.reshape(B, T, D), c, ct, H, tq=tk, tk=tk)
    out = _mm_res([o.reshape(B * T, D)], w_o.astype(BF16), x.reshape(B * T, D), lambda a: a, name="fx_out")
    return out.reshape(B, T, D)


def kernel(x, norm1_g, norm2_g, mlp_w1, mlp_w2, cc_w_in, cc_b_in, cc_dw, cc_dw_b, cc_ln_g, cc_ln_b, cc_w_out, cc_b_out, rw_mu, rw_w_rkv, rw_w0, rw_w1, rw_w2, rw_a0, rw_a1, rw_a2, rw_g1, rw_g2, rw_k_k, rw_k_a, rw_r_k, rw_gn_g, rw_gn_b, rw_w_o, sc_w_in, sc_conv_w, sc_w_out, fx_w_qkvf, fx_b_f, fx_w_o, final_g):
    B, T, D = x.shape
    depth = norm1_g.shape[0]
    n_mixers = 4
    for i in range(depth):
        m, j = i % n_mixers, i // n_mixers
        g = norm1_g[i]
        if m == 0:
            x = _conformer(x, g, cc_w_in[j], cc_b_in[j], cc_dw[j], cc_dw_b[j], cc_ln_g[j], cc_ln_b[j],
                           cc_w_out[j], cc_b_out[j])
        elif m == 1:
            x = _rwkv7(x, g, rw_mu[j], rw_w_rkv[j], rw_w0[j], rw_w1[j], rw_w2[j], rw_a0[j], rw_a1[j], rw_a2[j],
                       rw_g1[j], rw_g2[j], rw_k_k[j], rw_k_a[j], rw_r_k[j], rw_gn_g[j], rw_gn_b[j], rw_w_o[j])
        elif m == 2:
            x = _short_conv(x, g, sc_w_in[j], sc_conv_w[j], sc_w_out[j])
        else:
            x = _fox(x, g, fx_w_qkvf[j], fx_b_f[j], fx_w_o[j])
        x = _mlp(x.reshape(B * T, D), norm2_g[i], mlp_w1[i].astype(BF16), mlp_w2[i].astype(BF16),
                 final_g=final_g if i == depth - 1 else None).reshape(B, T, D)
    return x
```

```python
import functools
import math

import jax
import jax.numpy as jnp
from jax import lax
from jax.experimental import pallas as pl
from jax.experimental.pallas import tpu as pltpu

F32 = jnp.float32
BF16 = jnp.bfloat16

RMS_EPS = 1e-6
LN_EPS = 1e-5
GN_EPS = 64e-5
KK_EPS = 1e-12

LANES = 128
SUBLANES = 8
VMEM_LIMIT = 52 * 1024 * 1024

RWKV_CHUNK = 64
NEG_BIG = -1e30


def _cparams(*sem):
    return pltpu.CompilerParams(dimension_semantics=sem, vmem_limit_bytes=VMEM_LIMIT)


def _tile(n, want):
    t = min(n, want)
    while n % t:
        t -= 1
    return t


def _row(v):
    return v.reshape(1, -1).astype(F32)


def _rms(x, g):
    return x * lax.rsqrt(jnp.mean(x * x, axis=-1, keepdims=True) + RMS_EPS) * g


def _dot(a, b):
    return jnp.dot(a.astype(BF16), b.astype(BF16), preferred_element_type=F32)


def _dot_nt(a, b):
    return lax.dot_general(a.astype(BF16), b.astype(BF16), (((1,), (1,)), ((), ())),
                           preferred_element_type=F32)


def _dot_tn(a, b):
    return lax.dot_general(a.astype(BF16), b.astype(BF16), (((0,), (0,)), ((), ())),
                           preferred_element_type=F32)


def _softplus(z):
    return jnp.maximum(z, 0.0) + jnp.log1p(jnp.exp(-jnp.abs(z)))


def _sigmoid(z):
    return 1.0 / (1.0 + jnp.exp(-z))


def _tri_cumsum(tri, v):
    hi = v.astype(BF16)
    r1 = v - hi.astype(F32)
    mid = r1.astype(BF16)
    lo = (r1 - mid.astype(F32)).astype(BF16)
    dot = functools.partial(jnp.dot, preferred_element_type=F32)
    return dot(tri, hi) + dot(tri, mid) + dot(tri, lo)


def _norm_mm_kernel(*refs, n_w, n_b, n_out, epilogue):
    x_ref, g_ref = refs[:2]
    w_refs = refs[2:2 + n_w]
    b_refs = refs[2 + n_w:2 + n_w + n_b]
    o_refs = refs[2 + n_w + n_b:2 + n_w + n_b + n_out]
    xn_ref = refs[-1]

    @pl.when(pl.program_id(1) == 0)
    def _():
        xn_ref[...] = _rms(x_ref[...], g_ref[...]).astype(BF16)

    xn = xn_ref[...]
    ys = [jnp.dot(xn, w[...], preferred_element_type=F32) for w in w_refs]
    if n_b:
        ys = [y + b[...] for y, b in zip(ys, b_refs)]
    for o, val in zip(o_refs, epilogue(*ys)):
        o[...] = val.astype(o.dtype)


def _norm_mm(x, g, w, n_groups, epilogue, out_dtypes, biases=None, tm=512, tn=512, name=None):
    M, D = x.shape
    N = w.shape[1] // n_groups
    tm, tn = _tile(M, tm), _tile(N, tn)
    nb = N // tn
    w_specs = [pl.BlockSpec((D, tn), functools.partial(lambda i, j, q: (0, j + q * nb), q=q))
               for q in range(n_groups)]
    args = [x, _row(g)] + [w] * n_groups
    specs = [pl.BlockSpec((tm, D), lambda i, j: (i, 0)), pl.BlockSpec((1, D), lambda i, j: (0, 0))] + w_specs
    n_b = 0
    if biases is not None:
        n_b = n_groups
        args += [_row(biases)] * n_groups
        specs += [pl.BlockSpec((1, tn), functools.partial(lambda i, j, q: (0, j + q * nb), q=q))
                  for q in range(n_groups)]
    outs = pl.pallas_call(
        functools.partial(_norm_mm_kernel, n_w=n_groups, n_b=n_b, n_out=len(out_dtypes), epilogue=epilogue),
        grid=(M // tm, nb),
        in_specs=specs,
        out_specs=[pl.BlockSpec((tm, tn), lambda i, j: (i, j)) for _ in out_dtypes],
        out_shape=[jax.ShapeDtypeStruct((M, N), dt) for dt in out_dtypes],
        scratch_shapes=[pltpu.VMEM((tm, D), BF16)],
        compiler_params=_cparams("parallel", "arbitrary"),
        name=name,
    )(*args)
    return outs


def _mm_res_kernel(*refs, n_in, has_b, prologue):
    in_refs = refs[:n_in]
    w_ref = refs[n_in]
    b_ref = refs[n_in + 1] if has_b else None
    res_ref = refs[n_in + 1 + has_b]
    o_ref = refs[n_in + 2 + has_b]
    act_ref = refs[-1]

    @pl.when(pl.program_id(1) == 0)
    def _():
        act_ref[...] = prologue(*[r[...] for r in in_refs]).astype(BF16)

    y = jnp.dot(act_ref[...], w_ref[...], preferred_element_type=F32)
    if has_b:
        y = y + b_ref[...]
    o_ref[...] = res_ref[...] + y


def _mm_res(ins, w, res, prologue, bias=None, tm=512, tn=512, name=None):
    M, K = ins[0].shape
    N = w.shape[1]
    tm, tn = _tile(M, tm), _tile(N, tn)
    args = list(ins) + [w]
    specs = [pl.BlockSpec((tm, K), lambda i, j: (i, 0)) for _ in ins] + [pl.BlockSpec((K, tn), lambda i, j: (0, j))]
    if bias is not None:
        args.append(_row(bias))
        specs.append(pl.BlockSpec((1, tn), lambda i, j: (0, j)))
    args.append(res)
    specs.append(pl.BlockSpec((tm, tn), lambda i, j: (i, j)))
    return pl.pallas_call(
        functools.partial(_mm_res_kernel, n_in=len(ins), has_b=bias is not None, prologue=prologue),
        grid=(M // tm, N // tn),
        in_specs=specs,
        out_specs=pl.BlockSpec((tm, tn), lambda i, j: (i, j)),
        out_shape=jax.ShapeDtypeStruct((M, N), F32),
        scratch_shapes=[pltpu.VMEM((tm, K), BF16)],
        compiler_params=_cparams("parallel", "arbitrary"),
        name=name,
    )(*args)


def _mlp_kernel(*refs, final):
    x_ref, g_ref, w1_ref, w2_ref = refs[:4]
    fg_ref = refs[4] if final else None
    o_ref, xn_ref, acc_ref = refs[-3:]
    f = pl.program_id(1)

    @pl.when(f == 0)
    def _():
        xn_ref[...] = _rms(x_ref[...], g_ref[...]).astype(BF16)
        acc_ref[...] = jnp.zeros_like(acc_ref)

    h = jnp.dot(xn_ref[...], w1_ref[...], preferred_element_type=F32)
    h = jnp.square(jnp.maximum(h, 0.0)).astype(BF16)
    acc_ref[...] += jnp.dot(h, w2_ref[...], preferred_element_type=F32)

    @pl.when(f == pl.num_programs(1) - 1)
    def _():
        y = x_ref[...] + acc_ref[...]
        if final:
            y = _rms(y, fg_ref[...])
        o_ref[...] = y


def _mlp(x, g, w1, w2, final_g=None, tm=512, tf=1024):
    M, D = x.shape
    F = w1.shape[1]
    tm, tf = _tile(M, tm), _tile(F, tf)
    final = final_g is not None
    args = [x, _row(g), w1, w2]
    specs = [pl.BlockSpec((tm, D), lambda i, f: (i, 0)), pl.BlockSpec((1, D), lambda i, f: (0, 0)),
             pl.BlockSpec((D, tf), lambda i, f: (0, f)), pl.BlockSpec((tf, D), lambda i, f: (f, 0))]
    if final:
        args.append(_row(final_g))
        specs.append(pl.BlockSpec((1, D), lambda i, f: (0, 0)))
    return pl.pallas_call(
        functools.partial(_mlp_kernel, final=final),
        grid=(M // tm, F // tf),
        in_specs=specs,
        out_specs=pl.BlockSpec((tm, D), lambda i, f: (i, 0)),
        out_shape=jax.ShapeDtypeStruct((M, D), F32),
        scratch_shapes=[pltpu.VMEM((tm, D), BF16), pltpu.VMEM((tm, D), F32)],
        compiler_params=_cparams("parallel", "arbitrary"),
        name="mlp_final" if final else "mlp",
    )(*args)


def _conv_mm_kernel(*refs, n_par, has_b, halo, make_act):
    cur_refs = refs[:2]
    halo_ref = refs[2]
    par_refs = refs[3:3 + n_par]
    w_ref = refs[3 + n_par]
    b_ref = refs[4 + n_par] if has_b else None
    res_ref = refs[4 + n_par + has_b]
    o_ref = refs[5 + n_par + has_b]
    ubuf_ref, act_ref = refs[-2:]
    tt = act_ref.shape[0]

    @pl.when(pl.program_id(2) == 0)
    def _():
        first = pl.program_id(1) == 0
        ubuf_ref[0:halo, :] = jnp.where(first, 0.0, halo_ref[0])
        ubuf_ref[halo:halo + tt, :] = cur_refs[0][0]
        make_act(ubuf_ref, cur_refs[1], [p[...] for p in par_refs], act_ref)

    y = jnp.dot(act_ref[...], w_ref[...], preferred_element_type=F32)
    if has_b:
        y = y + b_ref[...]
    o_ref[0] = res_ref[0] + y


def _conv_mm(u, side, params, w, res, make_act, halo, bias=None, tt=256, tn=512, name=None):
    B, T, D = u.shape
    N = w.shape[1]
    tt, tn = _tile(T, tt), _tile(N, tn)
    hb = tt // halo
    args = [u, side, u] + list(params) + [w]
    specs = [pl.BlockSpec((1, tt, D), lambda b, t, j: (b, t, 0)),
             pl.BlockSpec((1, tt, D), lambda b, t, j: (b, t, 0)),
             pl.BlockSpec((1, halo, D), lambda b, t, j: (b, jnp.maximum(t * hb - 1, 0), 0))]
    specs += [pl.BlockSpec(p.shape, lambda b, t, j: (0, 0)) for p in params]
    specs.append(pl.BlockSpec((D, tn), lambda b, t, j: (0, j)))
    if bias is not None:
        args.append(_row(bias))
        specs.append(pl.BlockSpec((1, tn), lambda b, t, j: (0, j)))
    args.append(res)
    specs.append(pl.BlockSpec((1, tt, tn), lambda b, t, j: (b, t, j)))
    return pl.pallas_call(
        functools.partial(_conv_mm_kernel, n_par=len(params), has_b=bias is not None, halo=halo,
                          make_act=make_act),
        grid=(B, T // tt, N // tn),
        in_specs=specs,
        out_specs=pl.BlockSpec((1, tt, tn), lambda b, t, j: (b, t, j)),
        out_shape=jax.ShapeDtypeStruct((B, T, N), F32),
        scratch_shapes=[pltpu.VMEM((halo + tt, D), F32), pltpu.VMEM((tt, D), BF16)],
        compiler_params=_cparams("parallel", "parallel", "arbitrary"),
        name=name,
    )(*args)


def _conformer_act(ubuf_ref, _side_ref, params, act_ref, *, taps, halo, rb, cc):
    dw, dw_b, ln_g, ln_b = params
    tt, D = act_ref.shape
    off = halo - (taps - 1)
    for r0 in range(0, tt, rb):
        pieces = []
        for c0 in range(0, D, cc):
            acc = jnp.zeros((rb, cc), F32)
            for k in range(taps):
                acc = acc + dw[k:k + 1, c0:c0 + cc] * ubuf_ref[off + k + r0:off + k + r0 + rb, c0:c0 + cc]
            pieces.append(acc + dw_b[:, c0:c0 + cc])
        mu = sum(jnp.sum(p, axis=-1, keepdims=True) for p in pieces) / D
        var = sum(jnp.sum(jnp.square(p - mu), axis=-1, keepdims=True) for p in pieces) / D
        inv = lax.rsqrt(var + LN_EPS)
        for c0, p in zip(range(0, D, cc), pieces):
            z = (p - mu) * inv * ln_g[:, c0:c0 + cc] + ln_b[:, c0:c0 + cc]
            act_ref[r0:r0 + rb, c0:c0 + cc] = (z * _sigmoid(z)).astype(BF16)


def _shortconv_act(ubuf_ref, gate_ref, params, act_ref, *, taps, halo, rb):
    (cw,) = params
    tt, _ = act_ref.shape
    off = halo - (taps - 1)
    for r0 in range(0, tt, rb):
        acc = cw[0:1, :] * ubuf_ref[off + r0:off + r0 + rb, :]
        for k in range(1, taps):
            acc = acc + cw[k:k + 1, :] * ubuf_ref[off + k + r0:off + k + r0 + rb, :]
        act_ref[r0:r0 + rb, :] = (gate_ref[0, r0:r0 + rb, :] * acc).astype(BF16)


def _rwkv_prep_kernel(x_ref, halo_ref, g_ref, mu_ref, w0_ref, w1_ref, w2_ref, a0_ref, a1_ref, a2_ref,
                      g1_ref, g2_ref, xrkv_ref, wp_ref, a_ref, gate_ref):
    g = g_ref[...]
    h = _rms(x_ref[0], g)
    prev_last = _rms(halo_ref[0, SUBLANES - 1:SUBLANES, :], g)
    prev_last = jnp.where(pl.program_id(1) == 0, 0.0, prev_last)
    row = lax.broadcasted_iota(jnp.int32, h.shape, 0)
    hprev = jnp.where(row == 0, prev_last, pltpu.roll(h, 1, 0))
    xx = hprev - h
    mu = mu_ref[...]
    xrkv_ref[0, 0] = (h + xx * mu[0:1]).astype(BF16)
    xrkv_ref[1, 0] = (h + xx * mu[2:3]).astype(BF16)
    xrkv_ref[2, 0] = (h + xx * mu[3:4]).astype(BF16)
    xw = h + xx * mu[1:2]
    xa = h + xx * mu[4:5]
    xg = h + xx * mu[5:6]
    wp_ref[0] = w0_ref[...] + _dot(jnp.tanh(_dot(xw, w1_ref[...])), w2_ref[...])
    a_ref[0] = _sigmoid(a0_ref[...] + _dot(_dot(xa, a1_ref[...]), a2_ref[...]))
    gate_ref[0] = _dot(_sigmoid(_dot(xg, g1_ref[...])), g2_ref[...])


def _rwkv_prep(x, g, mu, w0, w1, w2, a0, a1, a2, g1, g2, tt=256):
    B, T, D = x.shape
    tt = _tile(T, tt)
    hb = tt // SUBLANES
    full = lambda a: pl.BlockSpec(a.shape, lambda b, t: (0,) * a.ndim)
    small = [_row(g), mu, _row(w0), w1, w2, _row(a0), a1, a2, g1, g2]
    tile = pl.BlockSpec((1, tt, D), lambda b, t: (b, t, 0))
    return pl.pallas_call(
        _rwkv_prep_kernel,
        grid=(B, T // tt),
        in_specs=[tile, pl.BlockSpec((1, SUBLANES, D), lambda b, t: (b, jnp.maximum(t * hb - 1, 0), 0))]
        + [full(a) for a in small],
        out_specs=[pl.BlockSpec((3, 1, tt, D), lambda b, t: (0, b, t, 0)), tile, tile, tile],
        out_shape=[jax.ShapeDtypeStruct((3, B, T, D), BF16)] + [jax.ShapeDtypeStruct((B, T, D), F32)] * 3,
        compiler_params=_cparams("parallel", "arbitrary"),
        name="rwkv_prep",
    )(x, x, *small)


def _bmm_kernel(x_ref, w_ref, o_ref):
    o_ref[0] = jnp.dot(x_ref[0], w_ref[0], preferred_element_type=F32)


def _bmm(x, w, tm=512, tn=1024):
    G, M, K = x.shape
    N = w.shape[2]
    tm, tn = _tile(M, tm), _tile(N, tn)
    return pl.pallas_call(
        _bmm_kernel,
        grid=(G, M // tm, N // tn),
        in_specs=[pl.BlockSpec((1, tm, K), lambda q, i, j: (q, i, 0)),
                  pl.BlockSpec((1, K, tn), lambda q, i, j: (q, 0, j))],
        out_specs=pl.BlockSpec((1, tm, tn), lambda q, i, j: (q, i, j)),
        out_shape=jax.ShapeDtypeStruct((G, M, N), F32),
        compiler_params=_cparams("parallel", "parallel", "arbitrary"),
        name="rwkv_rkv",
    )(x, w)


def _rwkv_scan_kernel(r_ref, k_ref, v_ref, wp_ref, a_ref, kk_ref, ka_ref, rk_ref, gg_ref, gb_ref,
                      o_ref, s_ref, *, hd, chunk):
    L = chunk
    tb, hw = r_ref.shape[1], r_ref.shape[2]

    @pl.when(pl.program_id(2) == 0)
    def _():
        s_ref[...] = jnp.zeros_like(s_ref)

    rowi = lax.broadcasted_iota(jnp.int32, (L, L), 0)
    coli = lax.broadcasted_iota(jnp.int32, (L, L), 1)
    tri_bf = (rowi >= coli).astype(BF16)
    row2 = lax.broadcasted_iota(jnp.int32, (L, 2 * L), 0)
    col2 = lax.broadcasted_iota(jnp.int32, (L, 2 * L), 1)
    col2 = jnp.where(col2 >= L, col2 - L, col2)
    strict2 = row2 > col2
    incl2 = row2 >= col2
    eye = (rowi == coli).astype(F32)

    kk_p, ka_p, rk_p, gg_p, gb_p = kk_ref[...], ka_ref[...], rk_ref[...], gg_ref[...], gb_ref[...]
    n_c, n_h = tb // L, hw // hd
    pairs = [(c, h) for c in range(n_c) for h in range(n_h)]

    v, at, rt, kt, bt, ke, be, w_last, bonus = ({} for _ in range(9))
    for c in range(n_c):
        rows = slice(c * L, (c + 1) * L)
        r_all, k_all, v_all, a_all = r_ref[0, rows, :], k_ref[0, rows, :], v_ref[0, rows, :], a_ref[0, rows, :]
        lw = -jnp.exp(-_softplus(-wp_ref[0, rows, :]) - 0.5)
        cum = _tri_cumsum(tri_bf, lw)
        cum_end = cum[L - 1:L, :]
        w_in, w_ex, w_inv = jnp.exp(cum), jnp.exp(cum - lw), jnp.exp(-cum)
        w_end, w_last_all = jnp.exp(cum_end - cum), jnp.exp(cum_end)
        kkv_all = k_all * kk_p
        k2_all = k_all * (1.0 + (a_all - 1.0) * ka_p)
        rt_all, kt_all, ke_all = r_all * w_in, k2_all * w_inv, k2_all * w_end
        bonus_all = r_all * k2_all * rk_p
        for h in range(n_h):
            sl = slice(h * hd, (h + 1) * hd)
            kkv = kkv_all[:, sl]
            kk = kkv / jnp.maximum(jnp.sqrt(jnp.sum(kkv * kkv, axis=-1, keepdims=True)), KK_EPS)
            b = kk * a_all[:, sl]
            p = (c, h)
            v[p], rt[p], kt[p], ke[p] = v_all[:, sl], rt_all[:, sl], kt_all[:, sl], ke_all[:, sl]
            at[p], bt[p], be[p] = -kk * w_ex[:, sl], b * w_inv[:, sl], b * w_end[:, sl]
            w_last[p] = w_last_all[:, sl]
            bonus[p] = jnp.sum(bonus_all[:, sl], axis=-1, keepdims=True)

    aa = {p: _dot_nt(jnp.concatenate([at[p], rt[p]], axis=0), jnp.concatenate([kt[p], bt[p]], axis=0))
          for p in pairs}
    top = {p: jnp.where(strict2, aa[p][:L], 0.0) for p in pairs}
    bot = {p: jnp.where(incl2, aa[p][L:], 0.0) for p in pairs}

    x = {p: top[p][:, L:] for p in pairs}
    t_inv = {p: eye + x[p] for p in pairs}
    for _ in range(int(math.log2(L)) - 1):
        x = {p: _dot(x[p], x[p]) for p in pairs}
        t_inv = {p: t_inv[p] + _dot(t_inv[p], x[p]) for p in pairs}

    t_at = {p: _dot(t_inv[p], at[p]) for p in pairs}
    q0 = {p: _dot(top[p][:, :L], v[p]) for p in pairs}
    q1 = {p: _dot(t_inv[p], q0[p]) for p in pairs}
    vq = {p: jnp.concatenate([v[p], q1[p]], axis=0) for p in pairs}
    r2 = {p: rt[p] + _dot(bot[p][:, L:], t_at[p]) for p in pairs}
    y0 = {p: _dot(bot[p], vq[p]) for p in pairs}
    m = {p: _dot_tn(t_at[p], be[p]) for p in pairs}
    s_add = {p: _dot_tn(vq[p], jnp.concatenate([ke[p], be[p]], axis=0)) for p in pairs}

    state = [s_ref[h] for h in range(n_h)]
    for c in range(n_c):
        rows = slice(c * L, (c + 1) * L)
        y = [y0[c, h] + _dot_nt(r2[c, h], state[h]) for h in range(n_h)]
        state = [state[h] * w_last[c, h] + _dot(state[h], m[c, h]) + s_add[c, h] for h in range(n_h)]
        for h in range(n_h):
            sl = slice(h * hd, (h + 1) * hd)
            mean = jnp.mean(y[h], axis=-1, keepdims=True)
            var = jnp.mean(jnp.square(y[h] - mean), axis=-1, keepdims=True)
            yn = (y[h] - mean) * lax.rsqrt(var + GN_EPS) * gg_p[:, sl] + gb_p[:, sl]
            o_ref[0, rows, sl] = yn + bonus[c, h] * v[c, h]
    for h in range(n_h):
        s_ref[h] = state[h]


def _rwkv_scan(rkv, wp, a, k_k, k_a, r_k, gn_g, gn_b, hd, tb=256, hw=2 * LANES):
    _, B, T, D = rkv.shape
    tb, hw = _tile(T, tb), _tile(D, hw)
    tile = pl.BlockSpec((1, tb, hw), lambda b, h, t: (b, t, h))
    par = pl.BlockSpec((1, hw), lambda b, h, t: (0, h))
    rkv_specs = [pl.BlockSpec((None, 1, tb, hw), functools.partial(lambda b, h, t, q: (q, b, t, h), q=q))
                 for q in range(3)]
    return pl.pallas_call(
        functools.partial(_rwkv_scan_kernel, hd=hd, chunk=_tile(tb, RWKV_CHUNK)),
        grid=(B, D // hw, T // tb),
        in_specs=rkv_specs + [tile] * 2 + [par] * 5,
        out_specs=tile,
        out_shape=jax.ShapeDtypeStruct((B, T, D), F32),
        scratch_shapes=[pltpu.VMEM((hw // hd, hd, hd), F32)],
        compiler_params=_cparams("parallel", "parallel", "arbitrary"),
        name="rwkv_scan",
    )(rkv, rkv, rkv, wp, a, _row(k_k), _row(k_a), _row(r_k), _row(gn_g), _row(gn_b))


def _fox_gate_kernel(x_ref, g_ref, wf_ref, bf_ref, c_ref, carry_ref):
    @pl.when(pl.program_id(1) == 0)
    def _():
        carry_ref[...] = jnp.zeros_like(carry_ref)

    tt = x_ref.shape[1]
    z = _dot(_rms(x_ref[0], g_ref[...]), wf_ref[...]) + bf_ref[...]
    log_f = -_softplus(-z)
    tri = (lax.broadcasted_iota(jnp.int32, (tt, tt), 0) >= lax.broadcasted_iota(jnp.int32, (tt, tt), 1))
    c = carry_ref[...] + _tri_cumsum(tri.astype(BF16), log_f)
    c_ref[0] = c
    carry_ref[...] = c[tt - 1:tt, :]


def _fox_gate(x, g, wf, bf, tt=256):
    B, T, D = x.shape
    tt = _tile(T, tt)
    return pl.pallas_call(
        _fox_gate_kernel,
        grid=(B, T // tt),
        in_specs=[pl.BlockSpec((1, tt, D), lambda b, t: (b, t, 0)), pl.BlockSpec((1, D), lambda b, t: (0, 0)),
                  pl.BlockSpec((D, LANES), lambda b, t: (0, 0)), pl.BlockSpec((1, LANES), lambda b, t: (0, 0))],
        out_specs=pl.BlockSpec((1, tt, LANES), lambda b, t: (b, t, 0)),
        out_shape=jax.ShapeDtypeStruct((B, T, LANES), F32),
        scratch_shapes=[pltpu.VMEM((1, LANES), F32)],
        compiler_params=_cparams("parallel", "arbitrary"),
        name="fox_gate",
    )(x, _row(g), wf, bf)


def _fox_attn_kernel(q_ref, k_ref, v_ref, cq_ref, ck_ref, o_ref, *, tq, tk):
    h, qi = pl.program_id(1), pl.program_id(2)
    q = q_ref[0]
    lane = lax.broadcasted_iota(jnp.int32, cq_ref.shape[1:], 1)
    cq = jnp.sum(jnp.where(lane == h, cq_ref[0], 0.0), axis=-1, keepdims=True)
    nd = tq // tk

    def step(kb, carry, diag):
        m, l, acc = carry
        ks = k_ref[0, pl.ds(pl.multiple_of(kb * tk, tk), tk), :]
        vs = v_ref[0, pl.ds(pl.multiple_of(kb * tk, tk), tk), :]
        s = lax.dot_general(q, ks, (((1,), (1,)), ((), ())), preferred_element_type=F32)
        s = s + (cq - ck_ref[0, 0, pl.ds(kb, 1), :])
        if diag is not None:
            rowi = lax.broadcasted_iota(jnp.int32, (tq, tk), 0)
            coli = lax.broadcasted_iota(jnp.int32, (tq, tk), 1) + diag * tk
            s = jnp.where(rowi >= coli, s, NEG_BIG)
        m_new = jnp.maximum(m, jnp.max(s, axis=-1, keepdims=True))
        alpha = jnp.exp(m - m_new)
        p = jnp.exp(s - m_new)
        l = alpha * l + jnp.sum(p, axis=-1, keepdims=True)
        acc = alpha * acc + jnp.dot(p.astype(BF16), vs, preferred_element_type=F32)
        return m_new, l, acc

    init = (jnp.full((tq, 1), NEG_BIG, F32), jnp.zeros((tq, 1), F32), jnp.zeros((tq, q.shape[1]), F32))
    carry = lax.fori_loop(0, qi * nd, lambda kb, cr: step(kb, cr, None), init)
    for d in range(nd):
        carry = step(qi * nd + d, carry, d)
    _, l, acc = carry
    o_ref[0] = (acc / l).astype(o_ref.dtype)


def _fox_attn(q, k, v, c, ct, n_heads, tq=512, tk=512):
    B, T, D = q.shape
    dh = D // n_heads
    return pl.pallas_call(
        functools.partial(_fox_attn_kernel, tq=tq, tk=tk),
        grid=(B, n_heads, T // tq),
        in_specs=[pl.BlockSpec((1, tq, dh), lambda b, h, i: (b, i, h)),
                  pl.BlockSpec((1, T, dh), lambda b, h, i: (b, 0, h)),
                  pl.BlockSpec((1, T, dh), lambda b, h, i: (b, 0, h)),
                  pl.BlockSpec((1, tq, LANES), lambda b, h, i: (b, i, 0)),
                  pl.BlockSpec((1, 1, T // tk, tk), lambda b, h, i: (b, h, 0, 0))],
        out_specs=pl.BlockSpec((1, tq, dh), lambda b, h, i: (b, i, h)),
        out_shape=jax.ShapeDtypeStruct((B, T, D), BF16),
        compiler_params=_cparams("parallel", "parallel", "arbitrary"),
        name="fox_attn",
    )(q, k, v, c, ct)


def _conformer(x, g, w_in, b_in, dw, dw_b, ln_g, ln_b, w_out, b_out):
    B, T, D = x.shape
    (u,) = _norm_mm(x.reshape(B * T, D), g, w_in.astype(BF16), 2,
                    lambda val, gate: (val * _sigmoid(gate),), [F32], biases=b_in, name="cc_in")
    taps = dw.shape[0]
    halo = -(-(taps - 1) // SUBLANES) * SUBLANES
    make_act = functools.partial(_conformer_act, taps=taps, halo=halo, rb=32, cc=_tile(D, 512))
    u = u.reshape(B, T, D)
    return _conv_mm(u, u, [dw, _row(dw_b), _row(ln_g), _row(ln_b)], w_out.astype(BF16), x, make_act, halo,
                    bias=b_out, name="cc_out")


def _short_conv(x, g, w_in, conv_w, w_out):
    B, T, D = x.shape
    gate_b, p = _norm_mm(x.reshape(B * T, D), g, w_in.astype(BF16), 3,
                         lambda gb, gc, hv: (gb, gc * hv), [F32, F32], name="sc_in")
    taps = conv_w.shape[0]
    halo = -(-(taps - 1) // SUBLANES) * SUBLANES
    make_act = functools.partial(_shortconv_act, taps=taps, halo=halo, rb=64)
    return _conv_mm(p.reshape(B, T, D), gate_b.reshape(B, T, D), [conv_w], w_out.astype(BF16), x, make_act,
                    halo, name="sc_out")


def _pad_cols(w, n):
    return jnp.pad(w, ((0, 0), (0, n - w.shape[1])))


def _pad_rows(w, n):
    return jnp.pad(w, ((0, n - w.shape[0]), (0, 0)))


def _rwkv7(x, g, mu, w_rkv, w0, w1, w2, a0, a1, a2, g1, g2, k_k, k_a, r_k, gn_g, gn_b, w_o):
    B, T, D = x.shape
    hd = r_k.shape[-1]
    rank = lambda w: -(-w.shape[1] // LANES) * LANES
    xrkv, wp, a, gate = _rwkv_prep(
        x, g, mu, w0,
        _pad_cols(w1, rank(w1)).astype(BF16), _pad_rows(w2, rank(w1)).astype(BF16), a0,
        _pad_cols(a1, rank(a1)).astype(BF16), _pad_rows(a2, rank(a1)).astype(BF16),
        _pad_cols(g1, rank(g1)).astype(BF16), _pad_rows(g2, rank(g1)).astype(BF16))
    rkv = _bmm(xrkv.reshape(3, B * T, D), w_rkv.astype(BF16)).reshape(3, B, T, D)
    y = _rwkv_scan(rkv, wp, a, k_k, k_a, r_k.reshape(-1), gn_g, gn_b, hd)
    out = _mm_res([y.reshape(B * T, D), gate.reshape(B * T, D)], w_o.astype(BF16), x.reshape(B * T, D),
                  lambda yy, gg: yy * gg, name="rwkv_out")
    return out.reshape(B, T, D)


def _fox(x, g, w_qkvf, b_f, w_o, tk=512):
    B, T, D = x.shape
    H = b_f.shape[0]
    scale = 1.0 / math.sqrt(D // H)
    q, k, v = _norm_mm(x.reshape(B * T, D), g, w_qkvf[:, :3 * D].astype(BF16), 3,
                       lambda q_, k_, v_: (q_ * scale, k_, v_), [BF16] * 3, name="fx_qkv")
    c = _fox_gate(x, g, _pad_cols(w_qkvf[:, 3 * D:], LANES).astype(BF16), _pad_cols(_row(b_f), LANES))
    tk = _tile(T, tk)
    ct = jnp.transpose(c[:, :, :H], (0, 2, 1)).reshape(B, H, T // tk, tk)
    o = _fox_attn(q.reshape(B, T, D), k.reshape(B, T, D), v.reshape(B, T, D), c, ct, H, tq=tk, tk=tk)
    out = _mm_res([o.reshape(B * T, D)], w_o.astype(BF16), x.reshape(B * T, D), lambda a: a, name="fx_out")
    return out.reshape(B, T, D)


def kernel(x, norm1_g, norm2_g, mlp_w1, mlp_w2, cc_w_in, cc_b_in, cc_dw, cc_dw_b, cc_ln_g, cc_ln_b, cc_w_out, cc_b_out, rw_mu, rw_w_rkv, rw_w0, rw_w1, rw_w2, rw_a0, rw_a1, rw_a2, rw_g1, rw_g2, rw_k_k, rw_k_a, rw_r_k, rw_gn_g, rw_gn_b, rw_w_o, sc_w_in, sc_conv_w, sc_w_out, fx_w_qkvf, fx_b_f, fx_w_o, final_g):
    B, T, D = x.shape
    depth = norm1_g.shape[0]
    n_mixers = 4
    for i in range(depth):
        m, j = i % n_mixers, i // n_mixers
        g = norm1_g[i]
        if m == 0:
            x = _conformer(x, g, cc_w_in[j], cc_b_in[j], cc_dw[j], cc_dw_b[j], cc_ln_g[j], cc_ln_b[j],
                           cc_w_out[j], cc_b_out[j])
        elif m == 1:
            x = _rwkv7(x, g, rw_mu[j], rw_w_rkv[j], rw_w0[j], rw_w1[j], rw_w2[j], rw_a0[j], rw_a1[j], rw_a2[j],
                       rw_g1[j], rw_g2[j], rw_k_k[j], rw_k_a[j], rw_r_k[j], rw_gn_g[j], rw_gn_b[j], rw_w_o[j])
        elif m == 2:
            x = _short_conv(x, g, sc_w_in[j], sc_conv_w[j], sc_w_out[j])
        else:
            x = _fox(x, g, fx_w_qkvf[j], fx_b_f[j], fx_w_o[j])
        x = _mlp(x.reshape(B * T, D), norm2_g[i], mlp_w1[i].astype(BF16), mlp_w2[i].astype(BF16),
                 final_g=final_g if i == depth - 1 else None).reshape(B, T, D)
    return x
```

```python
import functools
import math

import jax
import jax.numpy as jnp
from jax import lax
from jax.experimental import pallas as pl
from jax.experimental.pallas import tpu as pltpu

F32 = jnp.float32
BF16 = jnp.bfloat16

RMS_EPS = 1e-6
LN_EPS = 1e-5
GN_EPS = 64e-5
KK_EPS = 1e-12

LANES = 128
SUBLANES = 8
VMEM_LIMIT = 56 * 1024 * 1024

RWKV_CHUNK = 64
NEG_BIG = -1e30


def _cparams(*sem):
    return pltpu.CompilerParams(dimension_semantics=sem, vmem_limit_bytes=VMEM_LIMIT)


def _tile(n, want):
    t = min(n, want)
    while n % t:
        t -= 1
    return t


def _row(v):
    return v.reshape(1, -1).astype(F32)


def _rms(x, g):
    return x * lax.rsqrt(jnp.mean(x * x, axis=-1, keepdims=True) + RMS_EPS) * g


def _dot(a, b):
    return jnp.dot(a.astype(BF16), b.astype(BF16), preferred_element_type=F32)


def _dot_nt(a, b):
    return lax.dot_general(a.astype(BF16), b.astype(BF16), (((1,), (1,)), ((), ())),
                           preferred_element_type=F32)


def _dot_tn(a, b):
    return lax.dot_general(a.astype(BF16), b.astype(BF16), (((0,), (0,)), ((), ())),
                           preferred_element_type=F32)


def _softplus(z):
    return jnp.maximum(z, 0.0) + jnp.log1p(jnp.exp(-jnp.abs(z)))


def _sigmoid(z):
    return 1.0 / (1.0 + jnp.exp(-z))


def _tri_cumsum(tri, v):
    hi = v.astype(BF16)
    r1 = v - hi.astype(F32)
    mid = r1.astype(BF16)
    lo = (r1 - mid.astype(F32)).astype(BF16)
    dot = functools.partial(jnp.dot, preferred_element_type=F32)
    return dot(tri, hi) + dot(tri, mid) + dot(tri, lo)


def _norm_mm_kernel(*refs, n_w, n_b, n_out, epilogue):
    x_ref, g_ref = refs[:2]
    w_refs = refs[2:2 + n_w]
    b_refs = refs[2 + n_w:2 + n_w + n_b]
    o_refs = refs[2 + n_w + n_b:2 + n_w + n_b + n_out]
    xn_ref = refs[-1]

    @pl.when(pl.program_id(1) == 0)
    def _():
        xn_ref[...] = _rms(x_ref[...], g_ref[...]).astype(BF16)

    xn = xn_ref[...]
    ys = [jnp.dot(xn, w[...], preferred_element_type=F32) for w in w_refs]
    if n_b:
        ys = [y + b[...] for y, b in zip(ys, b_refs)]
    for o, val in zip(o_refs, epilogue(*ys)):
        o[...] = val.astype(o.dtype)


def _norm_mm(x, g, w, n_groups, epilogue, out_dtypes, biases=None, tm=512, tn=512, name=None):
    M, D = x.shape
    N = w.shape[1] // n_groups
    tm, tn = _tile(M, tm), _tile(N, tn)
    nb = N // tn
    w_specs = [pl.BlockSpec((D, tn), functools.partial(lambda i, j, q: (0, j + q * nb), q=q))
               for q in range(n_groups)]
    args = [x, _row(g)] + [w] * n_groups
    specs = [pl.BlockSpec((tm, D), lambda i, j: (i, 0)), pl.BlockSpec((1, D), lambda i, j: (0, 0))] + w_specs
    n_b = 0
    if biases is not None:
        n_b = n_groups
        args += [_row(biases)] * n_groups
        specs += [pl.BlockSpec((1, tn), functools.partial(lambda i, j, q: (0, j + q * nb), q=q))
                  for q in range(n_groups)]
    outs = pl.pallas_call(
        functools.partial(_norm_mm_kernel, n_w=n_groups, n_b=n_b, n_out=len(out_dtypes), epilogue=epilogue),
        grid=(M // tm, nb),
        in_specs=specs,
        out_specs=[pl.BlockSpec((tm, tn), lambda i, j: (i, j)) for _ in out_dtypes],
        out_shape=[jax.ShapeDtypeStruct((M, N), dt) for dt in out_dtypes],
        scratch_shapes=[pltpu.VMEM((tm, D), BF16)],
        compiler_params=_cparams("parallel", "arbitrary"),
        name=name,
    )(*args)
    return outs


def _mm_res_kernel(*refs, n_in, has_b, prologue):
    in_refs = refs[:n_in]
    w_ref = refs[n_in]
    b_ref = refs[n_in + 1] if has_b else None
    res_ref = refs[n_in + 1 + has_b]
    o_ref = refs[n_in + 2 + has_b]
    act_ref = refs[-1]

    @pl.when(pl.program_id(1) == 0)
    def _():
        act_ref[...] = prologue(*[r[...] for r in in_refs]).astype(BF16)

    y = jnp.dot(act_ref[...], w_ref[...], preferred_element_type=F32)
    if has_b:
        y = y + b_ref[...]
    o_ref[...] = res_ref[...] + y


def _mm_res(ins, w, res, prologue, bias=None, tm=512, tn=512, name=None):
    M, K = ins[0].shape
    N = w.shape[1]
    tm, tn = _tile(M, tm), _tile(N, tn)
    args = list(ins) + [w]
    specs = [pl.BlockSpec((tm, K), lambda i, j: (i, 0)) for _ in ins] + [pl.BlockSpec((K, tn), lambda i, j: (0, j))]
    if bias is not None:
        args.append(_row(bias))
        specs.append(pl.BlockSpec((1, tn), lambda i, j: (0, j)))
    args.append(res)
    specs.append(pl.BlockSpec((tm, tn), lambda i, j: (i, j)))
    return pl.pallas_call(
        functools.partial(_mm_res_kernel, n_in=len(ins), has_b=bias is not None, prologue=prologue),
        grid=(M // tm, N // tn),
        in_specs=specs,
        out_specs=pl.BlockSpec((tm, tn), lambda i, j: (i, j)),
        out_shape=jax.ShapeDtypeStruct((M, N), F32),
        scratch_shapes=[pltpu.VMEM((tm, K), BF16)],
        compiler_params=_cparams("parallel", "arbitrary"),
        name=name,
    )(*args)


def _mlp_kernel(*refs, final):
    x_ref, g_ref, w1_ref, w2_ref = refs[:4]
    fg_ref = refs[4] if final else None
    o_ref, xn_ref = refs[-2:]
    f = pl.program_id(1)

    @pl.when(f == 0)
    def _():
        xn_ref[...] = _rms(x_ref[...], g_ref[...]).astype(BF16)
        o_ref[...] = x_ref[...]

    h = jnp.dot(xn_ref[...], w1_ref[...], preferred_element_type=F32)
    h = jnp.square(jnp.maximum(h, 0.0)).astype(BF16)
    o_ref[...] += jnp.dot(h, w2_ref[...], preferred_element_type=F32)

    if final:
        @pl.when(f == pl.num_programs(1) - 1)
        def _():
            o_ref[...] = _rms(o_ref[...], fg_ref[...])


def _mlp(x, g, w1, w2, final_g=None, tm=1024, tf=512):
    M, D = x.shape
    F = w1.shape[1]
    tm, tf = _tile(M, tm), _tile(F, tf)
    final = final_g is not None
    args = [x, _row(g), w1, w2]
    specs = [pl.BlockSpec((tm, D), lambda i, f: (i, 0)), pl.BlockSpec((1, D), lambda i, f: (0, 0)),
             pl.BlockSpec((D, tf), lambda i, f: (0, f)), pl.BlockSpec((tf, D), lambda i, f: (f, 0))]
    if final:
        args.append(_row(final_g))
        specs.append(pl.BlockSpec((1, D), lambda i, f: (0, 0)))
    return pl.pallas_call(
        functools.partial(_mlp_kernel, final=final),
        grid=(M // tm, F // tf),
        in_specs=specs,
        out_specs=pl.BlockSpec((tm, D), lambda i, f: (i, 0)),
        out_shape=jax.ShapeDtypeStruct((M, D), F32),
        scratch_shapes=[pltpu.VMEM((tm, D), BF16)],
        compiler_params=_cparams("parallel", "arbitrary"),
        name="mlp_final" if final else "mlp",
    )(*args)


def _conv_mm_kernel(*refs, n_par, has_b, halo, make_act):
    cur_refs = refs[:2]
    halo_ref = refs[2]
    par_refs = refs[3:3 + n_par]
    w_ref = refs[3 + n_par]
    b_ref = refs[4 + n_par] if has_b else None
    res_ref = refs[4 + n_par + has_b]
    o_ref = refs[5 + n_par + has_b]
    ubuf_ref, act_ref = refs[6 + n_par + has_b:8 + n_par + has_b]
    extra = refs[8 + n_par + has_b:]
    tt = act_ref.shape[0]

    @pl.when(pl.program_id(2) == 0)
    def _():
        first = pl.program_id(1) == 0
        ubuf_ref[0:halo, :] = jnp.where(first, 0.0, halo_ref[0])
        ubuf_ref[halo:halo + tt, :] = cur_refs[0][0]
        make_act(ubuf_ref, cur_refs[1], [p[...] for p in par_refs], act_ref, *extra)

    y = jnp.dot(act_ref[...], w_ref[...], preferred_element_type=F32)
    if has_b:
        y = y + b_ref[...]
    o_ref[0] = res_ref[0] + y


def _conv_mm(u, side, params, w, res, make_act, halo, bias=None, tt=256, tn=512, extra_scratch=(), name=None):
    B, T, D = u.shape
    N = w.shape[1]
    tt, tn = _tile(T, tt), _tile(N, tn)
    hb = tt // halo
    args = [u, side, u] + list(params) + [w]
    specs = [pl.BlockSpec((1, tt, D), lambda b, t, j: (b, t, 0)),
             pl.BlockSpec((1, tt, D), lambda b, t, j: (b, t, 0)),
             pl.BlockSpec((1, halo, D), lambda b, t, j: (b, jnp.maximum(t * hb - 1, 0), 0))]
    specs += [pl.BlockSpec(p.shape, lambda b, t, j: (0, 0)) for p in params]
    specs.append(pl.BlockSpec((D, tn), lambda b, t, j: (0, j)))
    if bias is not None:
        args.append(_row(bias))
        specs.append(pl.BlockSpec((1, tn), lambda b, t, j: (0, j)))
    args.append(res)
    specs.append(pl.BlockSpec((1, tt, tn), lambda b, t, j: (b, t, j)))
    return pl.pallas_call(
        functools.partial(_conv_mm_kernel, n_par=len(params), has_b=bias is not None, halo=halo,
                          make_act=make_act),
        grid=(B, T // tt, N // tn),
        in_specs=specs,
        out_specs=pl.BlockSpec((1, tt, tn), lambda b, t, j: (b, t, j)),
        out_shape=jax.ShapeDtypeStruct((B, T, N), F32),
        scratch_shapes=[pltpu.VMEM((halo + tt, D), F32), pltpu.VMEM((tt, D), BF16)] + list(extra_scratch),
        compiler_params=_cparams("parallel", "parallel", "arbitrary"),
        name=name,
    )(*args)


def _conformer_act(ubuf_ref, _side_ref, params, act_ref, shift_ref, *, taps, halo, rb, cc):
    dw, dw_b, ln_g, ln_b = params
    tt, D = act_ref.shape
    off = halo - (taps - 1)
    n = halo + tt
    for c0 in range(0, D, cc):
        for s in range(SUBLANES):
            shift_ref[s, 0:n - s, :] = ubuf_ref[s:n, c0:c0 + cc]
        for r0 in range(0, tt, rb):
            acc = jnp.zeros((rb, cc), F32)
            for k in range(taps):
                s = (off + k) % SUBLANES
                q = off + k - s + r0
                acc = acc + dw[k:k + 1, c0:c0 + cc] * shift_ref[s, q:q + rb, :]
            ubuf_ref[halo + r0:halo + r0 + rb, c0:c0 + cc] = acc + dw_b[:, c0:c0 + cc]
    for r0 in range(0, tt, rb):
        cv = ubuf_ref[halo + r0:halo + r0 + rb, :]
        mu = jnp.mean(cv, axis=-1, keepdims=True)
        var = jnp.mean(jnp.square(cv - mu), axis=-1, keepdims=True)
        z = (cv - mu) * lax.rsqrt(var + LN_EPS) * ln_g + ln_b
        act_ref[r0:r0 + rb, :] = (z * _sigmoid(z)).astype(BF16)


def _shortconv_act(ubuf_ref, gate_ref, params, act_ref, *, taps, halo, rb):
    (cw,) = params
    tt, _ = act_ref.shape
    off = halo - (taps - 1)
    for r0 in range(0, tt, rb):
        acc = cw[0:1, :] * ubuf_ref[off + r0:off + r0 + rb, :]
        for k in range(1, taps):
            acc = acc + cw[k:k + 1, :] * ubuf_ref[off + k + r0:off + k + r0 + rb, :]
        act_ref[r0:r0 + rb, :] = (gate_ref[0, r0:r0 + rb, :] * acc).astype(BF16)


def _rwkv_prep_kernel(x_ref, halo_ref, g_ref, mu_ref, w0_ref, w1_ref, w2_ref, a0_ref, a1_ref, a2_ref,
                      g1_ref, g2_ref, xrkv_ref, wp_ref, a_ref, gate_ref):
    g = g_ref[...]
    h = _rms(x_ref[0], g)
    prev_last = _rms(halo_ref[0, SUBLANES - 1:SUBLANES, :], g)
    prev_last = jnp.where(pl.program_id(1) == 0, 0.0, prev_last)
    row = lax.broadcasted_iota(jnp.int32, h.shape, 0)
    hprev = jnp.where(row == 0, prev_last, pltpu.roll(h, 1, 0))
    xx = hprev - h
    mu = mu_ref[...]
    xrkv_ref[0, 0] = (h + xx * mu[0:1]).astype(BF16)
    xrkv_ref[1, 0] = (h + xx * mu[2:3]).astype(BF16)
    xrkv_ref[2, 0] = (h + xx * mu[3:4]).astype(BF16)
    xw = h + xx * mu[1:2]
    xa = h + xx * mu[4:5]
    xg = h + xx * mu[5:6]
    wp_ref[0] = w0_ref[...] + _dot(jnp.tanh(_dot(xw, w1_ref[...])), w2_ref[...])
    a_ref[0] = _sigmoid(a0_ref[...] + _dot(_dot(xa, a1_ref[...]), a2_ref[...]))
    gate_ref[0] = _dot(_sigmoid(_dot(xg, g1_ref[...])), g2_ref[...])


def _rwkv_prep(x, g, mu, w0, w1, w2, a0, a1, a2, g1, g2, tt=256):
    B, T, D = x.shape
    tt = _tile(T, tt)
    hb = tt // SUBLANES
    full = lambda a: pl.BlockSpec(a.shape, lambda b, t: (0,) * a.ndim)
    small = [_row(g), mu, _row(w0), w1, w2, _row(a0), a1, a2, g1, g2]
    tile = pl.BlockSpec((1, tt, D), lambda b, t: (b, t, 0))
    return pl.pallas_call(
        _rwkv_prep_kernel,
        grid=(B, T // tt),
        in_specs=[tile, pl.BlockSpec((1, SUBLANES, D), lambda b, t: (b, jnp.maximum(t * hb - 1, 0), 0))]
        + [full(a) for a in small],
        out_specs=[pl.BlockSpec((3, 1, tt, D), lambda b, t: (0, b, t, 0)), tile, tile, tile],
        out_shape=[jax.ShapeDtypeStruct((3, B, T, D), BF16)] + [jax.ShapeDtypeStruct((B, T, D), F32)] * 3,
        compiler_params=_cparams("parallel", "arbitrary"),
        name="rwkv_prep",
    )(x, x, *small)


def _bmm_kernel(x_ref, w_ref, o_ref):
    o_ref[0] = jnp.dot(x_ref[0], w_ref[0], preferred_element_type=F32)


def _bmm(x, w, tm=512, tn=1024):
    G, M, K = x.shape
    N = w.shape[2]
    tm, tn = _tile(M, tm), _tile(N, tn)
    return pl.pallas_call(
        _bmm_kernel,
        grid=(G, M // tm, N // tn),
        in_specs=[pl.BlockSpec((1, tm, K), lambda q, i, j: (q, i, 0)),
                  pl.BlockSpec((1, K, tn), lambda q, i, j: (q, 0, j))],
        out_specs=pl.BlockSpec((1, tm, tn), lambda q, i, j: (q, i, j)),
        out_shape=jax.ShapeDtypeStruct((G, M, N), F32),
        compiler_params=_cparams("parallel", "parallel", "arbitrary"),
        name="rwkv_rkv",
    )(x, w)


def _rwkv_scan_kernel(r_ref, k_ref, v_ref, wp_ref, a_ref, kk_ref, ka_ref, rk_ref, gg_ref, gb_ref,
                      o_ref, s_ref, *, hd, chunk):
    L = chunk
    tb, hw = r_ref.shape[1], r_ref.shape[2]

    @pl.when(pl.program_id(2) == 0)
    def _():
        s_ref[...] = jnp.zeros_like(s_ref)

    rowi = lax.broadcasted_iota(jnp.int32, (L, L), 0)
    coli = lax.broadcasted_iota(jnp.int32, (L, L), 1)
    tri_bf = (rowi >= coli).astype(BF16)
    row2 = lax.broadcasted_iota(jnp.int32, (L, 2 * L), 0)
    col2 = lax.broadcasted_iota(jnp.int32, (L, 2 * L), 1)
    col2 = jnp.where(col2 >= L, col2 - L, col2)
    strict2 = row2 > col2
    incl2 = row2 >= col2
    eye = (rowi == coli).astype(F32)

    kk_p, ka_p, rk_p, gg_p, gb_p = kk_ref[...], ka_ref[...], rk_ref[...], gg_ref[...], gb_ref[...]
    n_c, n_h = tb // L, hw // hd
    pairs = [(c, h) for c in range(n_c) for h in range(n_h)]

    v, at, rt, kt, bt, ke, be, w_last, bonus = ({} for _ in range(9))
    for c in range(n_c):
        rows = slice(c * L, (c + 1) * L)
        r_all, k_all, v_all, a_all = r_ref[0, rows, :], k_ref[0, rows, :], v_ref[0, rows, :], a_ref[0, rows, :]
        lw = -jnp.exp(-_softplus(-wp_ref[0, rows, :]) - 0.5)
        cum = _tri_cumsum(tri_bf, lw)
        cum_end = cum[L - 1:L, :]
        w_in, w_ex, w_inv = jnp.exp(cum), jnp.exp(cum - lw), jnp.exp(-cum)
        w_end, w_last_all = jnp.exp(cum_end - cum), jnp.exp(cum_end)
        kkv_all = k_all * kk_p
        k2_all = k_all * (1.0 + (a_all - 1.0) * ka_p)
        rt_all, kt_all, ke_all = r_all * w_in, k2_all * w_inv, k2_all * w_end
        bonus_all = r_all * k2_all * rk_p
        for h in range(n_h):
            sl = slice(h * hd, (h + 1) * hd)
            kkv = kkv_all[:, sl]
            kk = kkv / jnp.maximum(jnp.sqrt(jnp.sum(kkv * kkv, axis=-1, keepdims=True)), KK_EPS)
            b = kk * a_all[:, sl]
            p = (c, h)
            v[p], rt[p], kt[p], ke[p] = v_all[:, sl], rt_all[:, sl], kt_all[:, sl], ke_all[:, sl]
            at[p], bt[p], be[p] = -kk * w_ex[:, sl], b * w_inv[:, sl], b * w_end[:, sl]
            w_last[p] = w_last_all[:, sl]
            bonus[p] = jnp.sum(bonus_all[:, sl], axis=-1, keepdims=True)

    aa = {p: _dot_nt(jnp.concatenate([at[p], rt[p]], axis=0), jnp.concatenate([kt[p], bt[p]], axis=0))
          for p in pairs}
    top = {p: jnp.where(strict2, aa[p][:L], 0.0) for p in pairs}
    bot = {p: jnp.where(incl2, aa[p][L:], 0.0) for p in pairs}

    x = {p: top[p][:, L:] for p in pairs}
    t_inv = {p: eye + x[p] for p in pairs}
    for _ in range(int(math.log2(L)) - 1):
        x = {p: _dot(x[p], x[p]) for p in pairs}
        t_inv = {p: t_inv[p] + _dot(t_inv[p], x[p]) for p in pairs}

    t_at = {p: _dot(t_inv[p], at[p]) for p in pairs}
    q0 = {p: _dot(top[p][:, :L], v[p]) for p in pairs}
    q1 = {p: _dot(t_inv[p], q0[p]) for p in pairs}
    vq = {p: jnp.concatenate([v[p], q1[p]], axis=0) for p in pairs}
    r2 = {p: rt[p] + _dot(bot[p][:, L:], t_at[p]) for p in pairs}
    y0 = {p: _dot(bot[p], vq[p]) for p in pairs}
    m = {p: _dot_tn(t_at[p], be[p]) for p in pairs}
    s_add = {p: _dot_tn(vq[p], jnp.concatenate([ke[p], be[p]], axis=0)) for p in pairs}

    state = [s_ref[h] for h in range(n_h)]
    for c in range(n_c):
        rows = slice(c * L, (c + 1) * L)
        y = [y0[c, h] + _dot_nt(r2[c, h], state[h]) for h in range(n_h)]
        state = [state[h] * w_last[c, h] + _dot(state[h], m[c, h]) + s_add[c, h] for h in range(n_h)]
        for h in range(n_h):
            sl = slice(h * hd, (h + 1) * hd)
            mean = jnp.mean(y[h], axis=-1, keepdims=True)
            var = jnp.mean(jnp.square(y[h] - mean), axis=-1, keepdims=True)
            yn = (y[h] - mean) * lax.rsqrt(var + GN_EPS) * gg_p[:, sl] + gb_p[:, sl]
            o_ref[0, rows, sl] = yn + bonus[c, h] * v[c, h]
    for h in range(n_h):
        s_ref[h] = state[h]


def _rwkv_scan(rkv, wp, a, k_k, k_a, r_k, gn_g, gn_b, hd, tb=256, hw=2 * LANES):
    _, B, T, D = rkv.shape
    tb, hw = _tile(T, tb), _tile(D, hw)
    tile = pl.BlockSpec((1, tb, hw), lambda b, h, t: (b, t, h))
    par = pl.BlockSpec((1, hw), lambda b, h, t: (0, h))
    rkv_specs = [pl.BlockSpec((None, 1, tb, hw), functools.partial(lambda b, h, t, q: (q, b, t, h), q=q))
                 for q in range(3)]
    return pl.pallas_call(
        functools.partial(_rwkv_scan_kernel, hd=hd, chunk=_tile(tb, RWKV_CHUNK)),
        grid=(B, D // hw, T // tb),
        in_specs=rkv_specs + [tile] * 2 + [par] * 5,
        out_specs=tile,
        out_shape=jax.ShapeDtypeStruct((B, T, D), F32),
        scratch_shapes=[pltpu.VMEM((hw // hd, hd, hd), F32)],
        compiler_params=_cparams("parallel", "parallel", "arbitrary"),
        name="rwkv_scan",
    )(rkv, rkv, rkv, wp, a, _row(k_k), _row(k_a), _row(r_k), _row(gn_g), _row(gn_b))


def _fox_gate_kernel(x_ref, g_ref, wf_ref, bf_ref, c_ref, carry_ref):
    @pl.when(pl.program_id(1) == 0)
    def _():
        carry_ref[...] = jnp.zeros_like(carry_ref)

    tt = x_ref.shape[1]
    z = _dot(_rms(x_ref[0], g_ref[...]), wf_ref[...]) + bf_ref[...]
    log_f = -_softplus(-z)
    tri = (lax.broadcasted_iota(jnp.int32, (tt, tt), 0) >= lax.broadcasted_iota(jnp.int32, (tt, tt), 1))
    c = carry_ref[...] + _tri_cumsum(tri.astype(BF16), log_f)
    c_ref[0] = c
    carry_ref[...] = c[tt - 1:tt, :]


def _fox_gate(x, g, wf, bf, tt=256):
    B, T, D = x.shape
    tt = _tile(T, tt)
    return pl.pallas_call(
        _fox_gate_kernel,
        grid=(B, T // tt),
        in_specs=[pl.BlockSpec((1, tt, D), lambda b, t: (b, t, 0)), pl.BlockSpec((1, D), lambda b, t: (0, 0)),
                  pl.BlockSpec((D, LANES), lambda b, t: (0, 0)), pl.BlockSpec((1, LANES), lambda b, t: (0, 0))],
        out_specs=pl.BlockSpec((1, tt, LANES), lambda b, t: (b, t, 0)),
        out_shape=jax.ShapeDtypeStruct((B, T, LANES), F32),
        scratch_shapes=[pltpu.VMEM((1, LANES), F32)],
        compiler_params=_cparams("parallel", "arbitrary"),
        name="fox_gate",
    )(x, _row(g), wf, bf)


def _fox_attn_kernel(q_ref, k_ref, v_ref, cq_ref, ck_ref, o_ref, *, tq, tk):
    h, qi = pl.program_id(1), pl.program_id(2)
    q = q_ref[0]
    lane = lax.broadcasted_iota(jnp.int32, cq_ref.shape[1:], 1)
    cq = jnp.sum(jnp.where(lane == h, cq_ref[0], 0.0), axis=-1, keepdims=True)
    nd = tq // tk

    def step(kb, carry, diag):
        m, l, acc = carry
        ks = k_ref[0, pl.ds(pl.multiple_of(kb * tk, tk), tk), :]
        vs = v_ref[0, pl.ds(pl.multiple_of(kb * tk, tk), tk), :]
        s = lax.dot_general(q, ks, (((1,), (1,)), ((), ())), preferred_element_type=F32)
        s = s + (cq - ck_ref[0, 0, pl.ds(kb, 1), :])
        if diag is not None:
            rowi = lax.broadcasted_iota(jnp.int32, (tq, tk), 0)
            coli = lax.broadcasted_iota(jnp.int32, (tq, tk), 1) + diag * tk
            s = jnp.where(rowi >= coli, s, NEG_BIG)
        m_new = jnp.maximum(m, jnp.max(s, axis=-1, keepdims=True))
        alpha = jnp.exp(m - m_new)
        p = jnp.exp(s - m_new)
        l = alpha * l + jnp.sum(p, axis=-1, keepdims=True)
        acc = alpha * acc + jnp.dot(p.astype(BF16), vs, preferred_element_type=F32)
        return m_new, l, acc

    init = (jnp.full((tq, 1), NEG_BIG, F32), jnp.zeros((tq, 1), F32), jnp.zeros((tq, q.shape[1]), F32))
    carry = lax.fori_loop(0, qi * nd, lambda kb, cr: step(kb, cr, None), init)
    for d in range(nd):
        carry = step(qi * nd + d, carry, d)
    _, l, acc = carry
    o_ref[0] = (acc / l).astype(o_ref.dtype)


def _fox_attn(q, k, v, c, ct, n_heads, tq=512, tk=512):
    B, T, D = q.shape
    dh = D // n_heads
    return pl.pallas_call(
        functools.partial(_fox_attn_kernel, tq=tq, tk=tk),
        grid=(B, n_heads, T // tq),
        in_specs=[pl.BlockSpec((1, tq, dh), lambda b, h, i: (b, i, h)),
                  pl.BlockSpec((1, T, dh), lambda b, h, i: (b, 0, h)),
                  pl.BlockSpec((1, T, dh), lambda b, h, i: (b, 0, h)),
                  pl.BlockSpec((1, tq, LANES), lambda b, h, i: (b, i, 0)),
                  pl.BlockSpec((1, 1, T // tk, tk), lambda b, h, i: (b, h, 0, 0))],
        out_specs=pl.BlockSpec((1, tq, dh), lambda b, h, i: (b, i, h)),
        out_shape=jax.ShapeDtypeStruct((B, T, D), BF16),
        compiler_params=_cparams("parallel", "parallel", "arbitrary"),
        name="fox_attn",
    )(q, k, v, c, ct)


def _conformer(x, g, w_in, b_in, dw, dw_b, ln_g, ln_b, w_out, b_out):
    B, T, D = x.shape
    (u,) = _norm_mm(x.reshape(B * T, D), g, w_in.astype(BF16), 2,
                    lambda val, gate: (val * _sigmoid(gate),), [F32], biases=b_in, name="cc_in")
    taps = dw.shape[0]
    halo = -(-(taps - 1) // SUBLANES) * SUBLANES
    tt, cc = _tile(T, 256), _tile(D, 512)
    make_act = functools.partial(_conformer_act, taps=taps, halo=halo, rb=32, cc=cc)
    u = u.reshape(B, T, D)
    return _conv_mm(u, u, [dw, _row(dw_b), _row(ln_g), _row(ln_b)], w_out.astype(BF16), x, make_act, halo,
                    bias=b_out, tt=tt, extra_scratch=[pltpu.VMEM((SUBLANES, halo + tt, cc), F32)], name="cc_out")


def _short_conv(x, g, w_in, conv_w, w_out):
    B, T, D = x.shape
    gate_b, p = _norm_mm(x.reshape(B * T, D), g, w_in.astype(BF16), 3,
                         lambda gb, gc, hv: (gb, gc * hv), [F32, F32], name="sc_in")
    taps = conv_w.shape[0]
    halo = -(-(taps - 1) // SUBLANES) * SUBLANES
    make_act = functools.partial(_shortconv_act, taps=taps, halo=halo, rb=64)
    return _conv_mm(p.reshape(B, T, D), gate_b.reshape(B, T, D), [conv_w], w_out.astype(BF16), x, make_act,
                    halo, name="sc_out")


def _pad_cols(w, n):
    return jnp.pad(w, ((0, 0), (0, n - w.shape[1])))


def _pad_rows(w, n):
    return jnp.pad(w, ((0, n - w.shape[0]), (0, 0)))


def _rwkv7(x, g, mu, w_rkv, w0, w1, w2, a0, a1, a2, g1, g2, k_k, k_a, r_k, gn_g, gn_b, w_o):
    B, T, D = x.shape
    hd = r_k.shape[-1]
    rank = lambda w: -(-w.shape[1] // LANES) * LANES
    xrkv, wp, a, gate = _rwkv_prep(
        x, g, mu, w0,
        _pad_cols(w1, rank(w1)).astype(BF16), _pad_rows(w2, rank(w1)).astype(BF16), a0,
        _pad_cols(a1, rank(a1)).astype(BF16), _pad_rows(a2, rank(a1)).astype(BF16),
        _pad_cols(g1, rank(g1)).astype(BF16), _pad_rows(g2, rank(g1)).astype(BF16))
    rkv = _bmm(xrkv.reshape(3, B * T, D), w_rkv.astype(BF16)).reshape(3, B, T, D)
    y = _rwkv_scan(rkv, wp, a, k_k, k_a, r_k.reshape(-1), gn_g, gn_b, hd)
    out = _mm_res([y.reshape(B * T, D), gate.reshape(B * T, D)], w_o.astype(BF16), x.reshape(B * T, D),
                  lambda yy, gg: yy * gg, name="rwkv_out")
    return out.reshape(B, T, D)


def _fox(x, g, w_qkvf, b_f, w_o, tk=512):
    B, T, D = x.shape
    H = b_f.shape[0]
    scale = 1.0 / math.sqrt(D // H)
    q, k, v = _norm_mm(x.reshape(B * T, D), g, w_qkvf[:, :3 * D].astype(BF16), 3,
                       lambda q_, k_, v_: (q_ * scale, k_, v_), [BF16] * 3, name="fx_qkv")
    c = _fox_gate(x, g, _pad_cols(w_qkvf[:, 3 * D:], LANES).astype(BF16), _pad_cols(_row(b_f), LANES))
    tk = _tile(T, tk)
    ct = jnp.transpose(c[:, :, :H], (0, 2, 1)).reshape(B, H, T // tk, tk)
    o = _fox_attn(q.reshape(B, T, D), k.reshape(B, T, D), v.reshape(B, T, D), c, ct, H, tq=tk, tk=tk)
    out = _mm_res([o.reshape(B * T, D)], w_o.astype(BF16), x.reshape(B * T, D), lambda a: a, name="fx_out")
    return out.reshape(B, T, D)


def kernel(x, norm1_g, norm2_g, mlp_w1, mlp_w2, cc_w_in, cc_b_in, cc_dw, cc_dw_b, cc_ln_g, cc_ln_b, cc_w_out, cc_b_out, rw_mu, rw_w_rkv, rw_w0, rw_w1, rw_w2, rw_a0, rw_a1, rw_a2, rw_g1, rw_g2, rw_k_k, rw_k_a, rw_r_k, rw_gn_g, rw_gn_b, rw_w_o, sc_w_in, sc_conv_w, sc_w_out, fx_w_qkvf, fx_b_f, fx_w_o, final_g):
    B, T, D = x.shape
    depth = norm1_g.shape[0]
    n_mixers = 4
    for i in range(depth):
        m, j = i % n_mixers, i // n_mixers
        g = norm1_g[i]
        if m == 0:
            x = _conformer(x, g, cc_w_in[j], cc_b_in[j], cc_dw[j], cc_dw_b[j], cc_ln_g[j], cc_ln_b[j],
                           cc_w_out[j], cc_b_out[j])
        elif m == 1:
            x = _rwkv7(x, g, rw_mu[j], rw_w_rkv[j], rw_w0[j], rw_w1[j], rw_w2[j], rw_a0[j], rw_a1[j], rw_a2[j],
                       rw_g1[j], rw_g2[j], rw_k_k[j], rw_k_a[j], rw_r_k[j], rw_gn_g[j], rw_gn_b[j], rw_w_o[j])
        elif m == 2:
            x = _short_conv(x, g, sc_w_in[j], sc_conv_w[j], sc_w_out[j])
        else:
            x = _fox(x, g, fx_w_qkvf[j], fx_b_f[j], fx_w_o[j])
        x = _mlp(x.reshape(B * T, D), norm2_g[i], mlp_w1[i].astype(BF16), mlp_w2[i].astype(BF16),
                 final_g=final_g if i == depth - 1 else None).reshape(B, T, D)
    return x
```

```python
import functools
import math

import jax
import jax.numpy as jnp
from jax import lax
from jax.experimental import pallas as pl
from jax.experimental.pallas import tpu as pltpu

F32 = jnp.float32
BF16 = jnp.bfloat16

RMS_EPS = 1e-6
LN_EPS = 1e-5
GN_EPS = 64e-5
KK_EPS = 1e-12

LANES = 128
SUBLANES = 8
VMEM_LIMIT = 56 * 1024 * 1024

RWKV_CHUNK = 64
NEG_BIG = -1e30


def _cparams(*sem):
    return pltpu.CompilerParams(dimension_semantics=sem, vmem_limit_bytes=VMEM_LIMIT)


def _tile(n, want):
    t = min(n, want)
    while n % t:
        t -= 1
    return t


def _row(v):
    return v.reshape(1, -1).astype(F32)


def _rms(x, g):
    return x * lax.rsqrt(jnp.mean(x * x, axis=-1, keepdims=True) + RMS_EPS) * g


def _dot(a, b):
    return jnp.dot(a.astype(BF16), b.astype(BF16), preferred_element_type=F32)


def _dot_nt(a, b):
    return lax.dot_general(a.astype(BF16), b.astype(BF16), (((1,), (1,)), ((), ())),
                           preferred_element_type=F32)


def _dot_tn(a, b):
    return lax.dot_general(a.astype(BF16), b.astype(BF16), (((0,), (0,)), ((), ())),
                           preferred_element_type=F32)


def _softplus(z):
    return jnp.maximum(z, 0.0) + jnp.log1p(jnp.exp(-jnp.abs(z)))


def _sigmoid(z):
    return 1.0 / (1.0 + jnp.exp(-z))


def _tri_cumsum(tri, v):
    hi = v.astype(BF16)
    r1 = v - hi.astype(F32)
    mid = r1.astype(BF16)
    lo = (r1 - mid.astype(F32)).astype(BF16)
    dot = functools.partial(jnp.dot, preferred_element_type=F32)
    return dot(tri, hi) + dot(tri, mid) + dot(tri, lo)


def _norm_mm_kernel(*refs, n_w, n_b, n_out, epilogue):
    x_ref, g_ref = refs[:2]
    w_refs = refs[2:2 + n_w]
    b_refs = refs[2 + n_w:2 + n_w + n_b]
    o_refs = refs[2 + n_w + n_b:2 + n_w + n_b + n_out]
    xn_ref = refs[-1]

    @pl.when(pl.program_id(1) == 0)
    def _():
        xn_ref[...] = _rms(x_ref[...], g_ref[...]).astype(BF16)

    xn = xn_ref[...]
    ys = [jnp.dot(xn, w[...], preferred_element_type=F32) for w in w_refs]
    if n_b:
        ys = [y + b[...] for y, b in zip(ys, b_refs)]
    for o, val in zip(o_refs, epilogue(*ys)):
        o[...] = val.astype(o.dtype)


def _norm_mm(x, g, w, n_groups, epilogue, out_dtypes, biases=None, tm=512, tn=512, name=None):
    M, D = x.shape
    N = w.shape[1] // n_groups
    tm, tn = _tile(M, tm), _tile(N, tn)
    nb = N // tn
    w_specs = [pl.BlockSpec((D, tn), functools.partial(lambda i, j, q: (0, j + q * nb), q=q))
               for q in range(n_groups)]
    args = [x, _row(g)] + [w] * n_groups
    specs = [pl.BlockSpec((tm, D), lambda i, j: (i, 0)), pl.BlockSpec((1, D), lambda i, j: (0, 0))] + w_specs
    n_b = 0
    if biases is not None:
        n_b = n_groups
        args += [_row(biases)] * n_groups
        specs += [pl.BlockSpec((1, tn), functools.partial(lambda i, j, q: (0, j + q * nb), q=q))
                  for q in range(n_groups)]
    outs = pl.pallas_call(
        functools.partial(_norm_mm_kernel, n_w=n_groups, n_b=n_b, n_out=len(out_dtypes), epilogue=epilogue),
        grid=(M // tm, nb),
        in_specs=specs,
        out_specs=[pl.BlockSpec((tm, tn), lambda i, j: (i, j)) for _ in out_dtypes],
        out_shape=[jax.ShapeDtypeStruct((M, N), dt) for dt in out_dtypes],
        scratch_shapes=[pltpu.VMEM((tm, D), BF16)],
        compiler_params=_cparams("parallel", "arbitrary"),
        name=name,
    )(*args)
    return outs


def _mm_res_kernel(*refs, n_in, has_b, prologue):
    in_refs = refs[:n_in]
    w_ref = refs[n_in]
    b_ref = refs[n_in + 1] if has_b else None
    res_ref = refs[n_in + 1 + has_b]
    o_ref = refs[n_in + 2 + has_b]
    act_ref = refs[-1]

    @pl.when(pl.program_id(1) == 0)
    def _():
        act_ref[...] = prologue(*[r[...] for r in in_refs]).astype(BF16)

    y = jnp.dot(act_ref[...], w_ref[...], preferred_element_type=F32)
    if has_b:
        y = y + b_ref[...]
    o_ref[...] = res_ref[...] + y


def _mm_res(ins, w, res, prologue, bias=None, tm=512, tn=512, name=None):
    M, K = ins[0].shape
    N = w.shape[1]
    tm, tn = _tile(M, tm), _tile(N, tn)
    args = list(ins) + [w]
    specs = [pl.BlockSpec((tm, K), lambda i, j: (i, 0)) for _ in ins] + [pl.BlockSpec((K, tn), lambda i, j: (0, j))]
    if bias is not None:
        args.append(_row(bias))
        specs.append(pl.BlockSpec((1, tn), lambda i, j: (0, j)))
    args.append(res)
    specs.append(pl.BlockSpec((tm, tn), lambda i, j: (i, j)))
    return pl.pallas_call(
        functools.partial(_mm_res_kernel, n_in=len(ins), has_b=bias is not None, prologue=prologue),
        grid=(M // tm, N // tn),
        in_specs=specs,
        out_specs=pl.BlockSpec((tm, tn), lambda i, j: (i, j)),
        out_shape=jax.ShapeDtypeStruct((M, N), F32),
        scratch_shapes=[pltpu.VMEM((tm, K), BF16)],
        compiler_params=_cparams("parallel", "arbitrary"),
        name=name,
    )(*args)


def _mlp_kernel(*refs, final):
    x_ref, g_ref, w1_ref, w2_ref = refs[:4]
    fg_ref = refs[4] if final else None
    o_ref, xn_ref = refs[-2:]
    f = pl.program_id(1)

    @pl.when(f == 0)
    def _():
        xn_ref[...] = _rms(x_ref[...], g_ref[...]).astype(BF16)
        o_ref[...] = x_ref[...]

    h = jnp.dot(xn_ref[...], w1_ref[...], preferred_element_type=F32)
    h = jnp.square(jnp.maximum(h, 0.0)).astype(BF16)
    o_ref[...] += jnp.dot(h, w2_ref[...], preferred_element_type=F32)

    if final:
        @pl.when(f == pl.num_programs(1) - 1)
        def _():
            o_ref[...] = _rms(o_ref[...], fg_ref[...])


def _mlp(x, g, w1, w2, final_g=None, tm=1024, tf=512):
    M, D = x.shape
    F = w1.shape[1]
    tm, tf = _tile(M, tm), _tile(F, tf)
    final = final_g is not None
    args = [x, _row(g), w1, w2]
    specs = [pl.BlockSpec((tm, D), lambda i, f: (i, 0)), pl.BlockSpec((1, D), lambda i, f: (0, 0)),
             pl.BlockSpec((D, tf), lambda i, f: (0, f)), pl.BlockSpec((tf, D), lambda i, f: (f, 0))]
    if final:
        args.append(_row(final_g))
        specs.append(pl.BlockSpec((1, D), lambda i, f: (0, 0)))
    return pl.pallas_call(
        functools.partial(_mlp_kernel, final=final),
        grid=(M // tm, F // tf),
        in_specs=specs,
        out_specs=pl.BlockSpec((tm, D), lambda i, f: (i, 0)),
        out_shape=jax.ShapeDtypeStruct((M, D), F32),
        scratch_shapes=[pltpu.VMEM((tm, D), BF16)],
        compiler_params=_cparams("parallel", "arbitrary"),
        name="mlp_final" if final else "mlp",
    )(*args)


def _conv_mm_kernel(*refs, n_par, has_b, halo, make_act):
    cur_refs = refs[:2]
    halo_ref = refs[2]
    par_refs = refs[3:3 + n_par]
    w_ref = refs[3 + n_par]
    b_ref = refs[4 + n_par] if has_b else None
    res_ref = refs[4 + n_par + has_b]
    o_ref = refs[5 + n_par + has_b]
    ubuf_ref, act_ref = refs[6 + n_par + has_b:8 + n_par + has_b]
    extra = refs[8 + n_par + has_b:]
    tt = act_ref.shape[0]

    @pl.when(pl.program_id(2) == 0)
    def _():
        first = pl.program_id(1) == 0
        ubuf_ref[0:halo, :] = jnp.where(first, 0.0, halo_ref[0])
        ubuf_ref[halo:halo + tt, :] = cur_refs[0][0]
        make_act(ubuf_ref, cur_refs[1], [p[...] for p in par_refs], act_ref, *extra)

    y = jnp.dot(act_ref[...], w_ref[...], preferred_element_type=F32)
    if has_b:
        y = y + b_ref[...]
    o_ref[0] = res_ref[0] + y


def _conv_mm(u, side, params, w, res, make_act, halo, bias=None, tt=256, tn=512, extra_scratch=(), name=None):
    B, T, D = u.shape
    N = w.shape[1]
    tt, tn = _tile(T, tt), _tile(N, tn)
    hb = tt // halo
    args = [u, side, u] + list(params) + [w]
    specs = [pl.BlockSpec((1, tt, D), lambda b, t, j: (b, t, 0)),
             pl.BlockSpec((1, tt, D), lambda b, t, j: (b, t, 0)),
             pl.BlockSpec((1, halo, D), lambda b, t, j: (b, jnp.maximum(t * hb - 1, 0), 0))]
    specs += [pl.BlockSpec(p.shape, lambda b, t, j: (0, 0)) for p in params]
    specs.append(pl.BlockSpec((D, tn), lambda b, t, j: (0, j)))
    if bias is not None:
        args.append(_row(bias))
        specs.append(pl.BlockSpec((1, tn), lambda b, t, j: (0, j)))
    args.append(res)
    specs.append(pl.BlockSpec((1, tt, tn), lambda b, t, j: (b, t, j)))
    return pl.pallas_call(
        functools.partial(_conv_mm_kernel, n_par=len(params), has_b=bias is not None, halo=halo,
                          make_act=make_act),
        grid=(B, T // tt, N // tn),
        in_specs=specs,
        out_specs=pl.BlockSpec((1, tt, tn), lambda b, t, j: (b, t, j)),
        out_shape=jax.ShapeDtypeStruct((B, T, N), F32),
        scratch_shapes=[pltpu.VMEM((halo + tt, D), F32), pltpu.VMEM((tt, D), BF16)] + list(extra_scratch),
        compiler_params=_cparams("parallel", "parallel", "arbitrary"),
        name=name,
    )(*args)


def _conformer_act(ubuf_ref, _side_ref, params, act_ref, shift_ref, *, taps, halo, rb, cc):
    dw, dw_b, ln_g, ln_b = params
    tt, D = act_ref.shape
    off = halo - (taps - 1)
    n = halo + tt
    for c0 in range(0, D, cc):
        for s in range(SUBLANES):
            shift_ref[s, 0:n - s, :] = ubuf_ref[s:n, c0:c0 + cc]
        for r0 in range(0, tt, rb):
            acc = jnp.zeros((rb, cc), F32)
            for k in range(taps):
                s = (off + k) % SUBLANES
                q = off + k - s + r0
                acc = acc + dw[k:k + 1, c0:c0 + cc] * shift_ref[s, q:q + rb, :]
            ubuf_ref[halo + r0:halo + r0 + rb, c0:c0 + cc] = acc + dw_b[:, c0:c0 + cc]
    for r0 in range(0, tt, rb):
        cv = ubuf_ref[halo + r0:halo + r0 + rb, :]
        mu = jnp.mean(cv, axis=-1, keepdims=True)
        var = jnp.mean(jnp.square(cv - mu), axis=-1, keepdims=True)
        z = (cv - mu) * lax.rsqrt(var + LN_EPS) * ln_g + ln_b
        act_ref[r0:r0 + rb, :] = (z * _sigmoid(z)).astype(BF16)


def _shortconv_act(ubuf_ref, gate_ref, params, act_ref, *, taps, halo, rb):
    (cw,) = params
    tt, _ = act_ref.shape
    off = halo - (taps - 1)
    for r0 in range(0, tt, rb):
        acc = cw[0:1, :] * ubuf_ref[off + r0:off + r0 + rb, :]
        for k in range(1, taps):
            acc = acc + cw[k:k + 1, :] * ubuf_ref[off + k + r0:off + k + r0 + rb, :]
        act_ref[r0:r0 + rb, :] = (gate_ref[0, r0:r0 + rb, :] * acc).astype(BF16)


def _rwkv_prep_kernel(x_ref, halo_ref, g_ref, mu_ref, w0_ref, w1_ref, w2_ref, a0_ref, a1_ref, a2_ref,
                      g1_ref, g2_ref, xrkv_ref, wp_ref, a_ref, gate_ref):
    g = g_ref[...]
    h = _rms(x_ref[0], g)
    prev_last = _rms(halo_ref[0, SUBLANES - 1:SUBLANES, :], g)
    prev_last = jnp.where(pl.program_id(1) == 0, 0.0, prev_last)
    row = lax.broadcasted_iota(jnp.int32, h.shape, 0)
    hprev = jnp.where(row == 0, prev_last, pltpu.roll(h, 1, 0))
    xx = hprev - h
    mu = mu_ref[...]
    xrkv_ref[0, 0] = (h + xx * mu[0:1]).astype(BF16)
    xrkv_ref[1, 0] = (h + xx * mu[2:3]).astype(BF16)
    xrkv_ref[2, 0] = (h + xx * mu[3:4]).astype(BF16)
    xw = h + xx * mu[1:2]
    xa = h + xx * mu[4:5]
    xg = h + xx * mu[5:6]
    wp_ref[0] = w0_ref[...] + _dot(jnp.tanh(_dot(xw, w1_ref[...])), w2_ref[...])
    a_ref[0] = _sigmoid(a0_ref[...] + _dot(_dot(xa, a1_ref[...]), a2_ref[...]))
    gate_ref[0] = _dot(_sigmoid(_dot(xg, g1_ref[...])), g2_ref[...])


def _rwkv_prep(x, g, mu, w0, w1, w2, a0, a1, a2, g1, g2, tt=256):
    B, T, D = x.shape
    tt = _tile(T, tt)
    hb = tt // SUBLANES
    full = lambda a: pl.BlockSpec(a.shape, lambda b, t: (0,) * a.ndim)
    small = [_row(g), mu, _row(w0), w1, w2, _row(a0), a1, a2, g1, g2]
    tile = pl.BlockSpec((1, tt, D), lambda b, t: (b, t, 0))
    return pl.pallas_call(
        _rwkv_prep_kernel,
        grid=(B, T // tt),
        in_specs=[tile, pl.BlockSpec((1, SUBLANES, D), lambda b, t: (b, jnp.maximum(t * hb - 1, 0), 0))]
        + [full(a) for a in small],
        out_specs=[pl.BlockSpec((3, 1, tt, D), lambda b, t: (0, b, t, 0)), tile, tile, tile],
        out_shape=[jax.ShapeDtypeStruct((3, B, T, D), BF16)] + [jax.ShapeDtypeStruct((B, T, D), F32)] * 3,
        compiler_params=_cparams("parallel", "arbitrary"),
        name="rwkv_prep",
    )(x, x, *small)


def _bmm_kernel(x_ref, w_ref, o_ref):
    o_ref[0] = jnp.dot(x_ref[0], w_ref[0], preferred_element_type=F32)


def _bmm(x, w, tm=512, tn=1024):
    G, M, K = x.shape
    N = w.shape[2]
    tm, tn = _tile(M, tm), _tile(N, tn)
    return pl.pallas_call(
        _bmm_kernel,
        grid=(G, M // tm, N // tn),
        in_specs=[pl.BlockSpec((1, tm, K), lambda q, i, j: (q, i, 0)),
                  pl.BlockSpec((1, K, tn), lambda q, i, j: (q, 0, j))],
        out_specs=pl.BlockSpec((1, tm, tn), lambda q, i, j: (q, i, j)),
        out_shape=jax.ShapeDtypeStruct((G, M, N), F32),
        compiler_params=_cparams("parallel", "parallel", "arbitrary"),
        name="rwkv_rkv",
    )(x, w)


def _rwkv_scan_kernel(r_ref, k_ref, v_ref, wp_ref, a_ref, kk_ref, ka_ref, rk_ref, gg_ref, gb_ref,
                      o_ref, s_ref, *, hd, chunk):
    L = chunk
    tb, hw = r_ref.shape[1], r_ref.shape[2]

    @pl.when(pl.program_id(2) == 0)
    def _():
        s_ref[...] = jnp.zeros_like(s_ref)

    rowi = lax.broadcasted_iota(jnp.int32, (L, L), 0)
    coli = lax.broadcasted_iota(jnp.int32, (L, L), 1)
    tri_bf = (rowi >= coli).astype(BF16)
    row2 = lax.broadcasted_iota(jnp.int32, (L, 2 * L), 0)
    col2 = lax.broadcasted_iota(jnp.int32, (L, 2 * L), 1)
    col2 = jnp.where(col2 >= L, col2 - L, col2)
    strict2 = row2 > col2
    incl2 = row2 >= col2
    eye = (rowi == coli).astype(F32)

    kk_p, ka_p, rk_p, gg_p, gb_p = kk_ref[...], ka_ref[...], rk_ref[...], gg_ref[...], gb_ref[...]
    n_c, n_h = tb // L, hw // hd
    pairs = [(c, h) for c in range(n_c) for h in range(n_h)]

    v, at, rt, kt, bt, ke, be, w_last, bonus = ({} for _ in range(9))
    for c in range(n_c):
        rows = slice(c * L, (c + 1) * L)
        r_all, k_all, v_all, a_all = r_ref[0, rows, :], k_ref[0, rows, :], v_ref[0, rows, :], a_ref[0, rows, :]
        lw = -jnp.exp(-_softplus(-wp_ref[0, rows, :]) - 0.5)
        cum = _tri_cumsum(tri_bf, lw)
        cum_end = cum[L - 1:L, :]
        w_in, w_ex, w_inv = jnp.exp(cum), jnp.exp(cum - lw), jnp.exp(-cum)
        w_end, w_last_all = jnp.exp(cum_end - cum), jnp.exp(cum_end)
        kkv_all = k_all * kk_p
        k2_all = k_all * (1.0 + (a_all - 1.0) * ka_p)
        rt_all, kt_all, ke_all = r_all * w_in, k2_all * w_inv, k2_all * w_end
        bonus_all = r_all * k2_all * rk_p
        for h in range(n_h):
            sl = slice(h * hd, (h + 1) * hd)
            kkv = kkv_all[:, sl]
            kk = kkv / jnp.maximum(jnp.sqrt(jnp.sum(kkv * kkv, axis=-1, keepdims=True)), KK_EPS)
            b = kk * a_all[:, sl]
            p = (c, h)
            v[p], rt[p], kt[p], ke[p] = v_all[:, sl], rt_all[:, sl], kt_all[:, sl], ke_all[:, sl]
            at[p], bt[p], be[p] = -kk * w_ex[:, sl], b * w_inv[:, sl], b * w_end[:, sl]
            w_last[p] = w_last_all[:, sl]
            bonus[p] = jnp.sum(bonus_all[:, sl], axis=-1, keepdims=True)

    aa = {p: _dot_nt(jnp.concatenate([at[p], rt[p]], axis=0), jnp.concatenate([kt[p], bt[p]], axis=0))
          for p in pairs}
    top = {p: jnp.where(strict2, aa[p][:L], 0.0) for p in pairs}
    bot = {p: jnp.where(incl2, aa[p][L:], 0.0) for p in pairs}

    x = {p: top[p][:, L:] for p in pairs}
    t_inv = {p: eye + x[p] for p in pairs}
    for _ in range(int(math.log2(L)) - 1):
        x = {p: _dot(x[p], x[p]) for p in pairs}
        t_inv = {p: t_inv[p] + _dot(t_inv[p], x[p]) for p in pairs}

    t_at = {p: _dot(t_inv[p], at[p]) for p in pairs}
    q0 = {p: _dot(top[p][:, :L], v[p]) for p in pairs}
    q1 = {p: _dot(t_inv[p], q0[p]) for p in pairs}
    vq = {p: jnp.concatenate([v[p], q1[p]], axis=0) for p in pairs}
    r2 = {p: rt[p] + _dot(bot[p][:, L:], t_at[p]) for p in pairs}
    y0 = {p: _dot(bot[p], vq[p]) for p in pairs}
    m = {p: _dot_tn(t_at[p], be[p]) for p in pairs}
    s_add = {p: _dot_tn(vq[p], jnp.concatenate([ke[p], be[p]], axis=0)) for p in pairs}

    state = [s_ref[h] for h in range(n_h)]
    for c in range(n_c):
        rows = slice(c * L, (c + 1) * L)
        y = [y0[c, h] + _dot_nt(r2[c, h], state[h]) for h in range(n_h)]
        state = [state[h] * w_last[c, h] + _dot(state[h], m[c, h]) + s_add[c, h] for h in range(n_h)]
        for h in range(n_h):
            sl = slice(h * hd, (h + 1) * hd)
            mean = jnp.mean(y[h], axis=-1, keepdims=True)
            var = jnp.mean(jnp.square(y[h] - mean), axis=-1, keepdims=True)
            yn = (y[h] - mean) * lax.rsqrt(var + GN_EPS) * gg_p[:, sl] + gb_p[:, sl]
            o_ref[0, rows, sl] = yn + bonus[c, h] * v[c, h]
    for h in range(n_h):
        s_ref[h] = state[h]


def _rwkv_scan(rkv, wp, a, k_k, k_a, r_k, gn_g, gn_b, hd, tb=256, hw=2 * LANES):
    _, B, T, D = rkv.shape
    tb, hw = _tile(T, tb), _tile(D, hw)
    tile = pl.BlockSpec((1, tb, hw), lambda b, h, t: (b, t, h))
    par = pl.BlockSpec((1, hw), lambda b, h, t: (0, h))
    rkv_specs = [pl.BlockSpec((None, 1, tb, hw), functools.partial(lambda b, h, t, q: (q, b, t, h), q=q))
                 for q in range(3)]
    return pl.pallas_call(
        functools.partial(_rwkv_scan_kernel, hd=hd, chunk=_tile(tb, RWKV_CHUNK)),
        grid=(B, D // hw, T // tb),
        in_specs=rkv_specs + [tile] * 2 + [par] * 5,
        out_specs=tile,
        out_shape=jax.ShapeDtypeStruct((B, T, D), F32),
        scratch_shapes=[pltpu.VMEM((hw // hd, hd, hd), F32)],
        compiler_params=_cparams("parallel", "parallel", "arbitrary"),
        name="rwkv_scan",
    )(rkv, rkv, rkv, wp, a, _row(k_k), _row(k_a), _row(r_k), _row(gn_g), _row(gn_b))


def _fox_gate_kernel(x_ref, g_ref, wf_ref, bf_ref, c_ref, carry_ref):
    @pl.when(pl.program_id(1) == 0)
    def _():
        carry_ref[...] = jnp.zeros_like(carry_ref)

    tt = x_ref.shape[1]
    z = _dot(_rms(x_ref[0], g_ref[...]), wf_ref[...]) + bf_ref[...]
    log_f = -_softplus(-z)
    tri = (lax.broadcasted_iota(jnp.int32, (tt, tt), 0) >= lax.broadcasted_iota(jnp.int32, (tt, tt), 1))
    c = carry_ref[...] + _tri_cumsum(tri.astype(BF16), log_f)
    c_ref[0] = c
    carry_ref[...] = c[tt - 1:tt, :]


def _fox_gate(x, g, wf, bf, tt=256):
    B, T, D = x.shape
    tt = _tile(T, tt)
    return pl.pallas_call(
        _fox_gate_kernel,
        grid=(B, T // tt),
        in_specs=[pl.BlockSpec((1, tt, D), lambda b, t: (b, t, 0)), pl.BlockSpec((1, D), lambda b, t: (0, 0)),
                  pl.BlockSpec((D, LANES), lambda b, t: (0, 0)), pl.BlockSpec((1, LANES), lambda b, t: (0, 0))],
        out_specs=pl.BlockSpec((1, tt, LANES), lambda b, t: (b, t, 0)),
        out_shape=jax.ShapeDtypeStruct((B, T, LANES), F32),
        scratch_shapes=[pltpu.VMEM((1, LANES), F32)],
        compiler_params=_cparams("parallel", "arbitrary"),
        name="fox_gate",
    )(x, _row(g), wf, bf)


def _fox_attn_kernel(q_ref, k_ref, v_ref, cq_ref, ck_ref, o_ref, *, tq, tk):
    h, qi = pl.program_id(1), pl.program_id(2)
    q = q_ref[0]
    lane = lax.broadcasted_iota(jnp.int32, cq_ref.shape[1:], 1)
    cq = jnp.sum(jnp.where(lane == h, cq_ref[0], 0.0), axis=-1, keepdims=True)
    nd = tq // tk

    def step(kb, carry, diag):
        m, l, acc = carry
        ks = k_ref[0, pl.ds(pl.multiple_of(kb * tk, tk), tk), :]
        vs = v_ref[0, pl.ds(pl.multiple_of(kb * tk, tk), tk), :]
        s = lax.dot_general(q, ks, (((1,), (1,)), ((), ())), preferred_element_type=F32)
        s = s + (cq - ck_ref[0, 0, pl.ds(kb, 1), :])
        if diag is not None:
            rowi = lax.broadcasted_iota(jnp.int32, (tq, tk), 0)
            coli = lax.broadcasted_iota(jnp.int32, (tq, tk), 1) + diag * tk
            s = jnp.where(rowi >= coli, s, NEG_BIG)
        m_new = jnp.maximum(m, jnp.max(s, axis=-1, keepdims=True))
        alpha = jnp.exp(m - m_new)
        p = jnp.exp(s - m_new)
        l = alpha * l + jnp.sum(p, axis=-1, keepdims=True)
        acc = alpha * acc + jnp.dot(p.astype(BF16), vs, preferred_element_type=F32)
        return m_new, l, acc

    init = (jnp.full((tq, 1), NEG_BIG, F32), jnp.zeros((tq, 1), F32), jnp.zeros((tq, q.shape[1]), F32))
    carry = lax.fori_loop(0, qi * nd, lambda kb, cr: step(kb, cr, None), init)
    for d in range(nd):
        carry = step(qi * nd + d, carry, d)
    _, l, acc = carry
    o_ref[0] = (acc / l).astype(o_ref.dtype)


def _fox_attn(q, k, v, c, ct, n_heads, tq=512, tk=512):
    B, T, D = q.shape
    dh = D // n_heads
    return pl.pallas_call(
        functools.partial(_fox_attn_kernel, tq=tq, tk=tk),
        grid=(B, n_heads, T // tq),
        in_specs=[pl.BlockSpec((1, tq, dh), lambda b, h, i: (b, i, h)),
                  pl.BlockSpec((1, T, dh), lambda b, h, i: (b, 0, h)),
                  pl.BlockSpec((1, T, dh), lambda b, h, i: (b, 0, h)),
                  pl.BlockSpec((1, tq, LANES), lambda b, h, i: (b, i, 0)),
                  pl.BlockSpec((1, 1, T // tk, tk), lambda b, h, i: (b, h, 0, 0))],
        out_specs=pl.BlockSpec((1, tq, dh), lambda b, h, i: (b, i, h)),
        out_shape=jax.ShapeDtypeStruct((B, T, D), BF16),
        compiler_params=_cparams("parallel", "parallel", "arbitrary"),
        name="fox_attn",
    )(q, k, v, c, ct)


def _conformer(x, g, w_in, b_in, dw, dw_b, ln_g, ln_b, w_out, b_out):
    B, T, D = x.shape
    (u,) = _norm_mm(x.reshape(B * T, D), g, w_in.astype(BF16), 2,
                    lambda val, gate: (val * _sigmoid(gate),), [F32], biases=b_in, name="cc_in")
    taps = dw.shape[0]
    halo = -(-(taps - 1) // SUBLANES) * SUBLANES
    tt, cc = _tile(T, 256), _tile(D, 512)
    make_act = functools.partial(_conformer_act, taps=taps, halo=halo, rb=32, cc=cc)
    u = u.reshape(B, T, D)
    return _conv_mm(u, u, [dw, _row(dw_b), _row(ln_g), _row(ln_b)], w_out.astype(BF16), x, make_act, halo,
                    bias=b_out, tt=tt, tn=D, extra_scratch=[pltpu.VMEM((SUBLANES, halo + tt, cc), F32)],
                    name="cc_out")


def _short_conv(x, g, w_in, conv_w, w_out):
    B, T, D = x.shape
    gate_b, p = _norm_mm(x.reshape(B * T, D), g, w_in.astype(BF16), 3,
                         lambda gb, gc, hv: (gb, gc * hv), [F32, F32], name="sc_in")
    taps = conv_w.shape[0]
    halo = -(-(taps - 1) // SUBLANES) * SUBLANES
    make_act = functools.partial(_shortconv_act, taps=taps, halo=halo, rb=64)
    return _conv_mm(p.reshape(B, T, D), gate_b.reshape(B, T, D), [conv_w], w_out.astype(BF16), x, make_act,
                    halo, tn=D, name="sc_out")


def _pad_cols(w, n):
    return jnp.pad(w, ((0, 0), (0, n - w.shape[1])))


def _pad_rows(w, n):
    return jnp.pad(w, ((0, n - w.shape[0]), (0, 0)))


def _rwkv7(x, g, mu, w_rkv, w0, w1, w2, a0, a1, a2, g1, g2, k_k, k_a, r_k, gn_g, gn_b, w_o):
    B, T, D = x.shape
    hd = r_k.shape[-1]
    rank = lambda w: -(-w.shape[1] // LANES) * LANES
    xrkv, wp, a, gate = _rwkv_prep(
        x, g, mu, w0,
        _pad_cols(w1, rank(w1)).astype(BF16), _pad_rows(w2, rank(w1)).astype(BF16), a0,
        _pad_cols(a1, rank(a1)).astype(BF16), _pad_rows(a2, rank(a1)).astype(BF16),
        _pad_cols(g1, rank(g1)).astype(BF16), _pad_rows(g2, rank(g1)).astype(BF16))
    rkv = _bmm(xrkv.reshape(3, B * T, D), w_rkv.astype(BF16)).reshape(3, B, T, D)
    y = _rwkv_scan(rkv, wp, a, k_k, k_a, r_k.reshape(-1), gn_g, gn_b, hd)
    out = _mm_res([y.reshape(B * T, D), gate.reshape(B * T, D)], w_o.astype(BF16), x.reshape(B * T, D),
                  lambda yy, gg: yy * gg, tm=256, tn=D, name="rwkv_out")
    return out.reshape(B, T, D)


def _fox(x, g, w_qkvf, b_f, w_o, tk=512):
    B, T, D = x.shape
    H = b_f.shape[0]
    scale = 1.0 / math.sqrt(D // H)
    q, k, v = _norm_mm(x.reshape(B * T, D), g, w_qkvf[:, :3 * D].astype(BF16), 3,
                       lambda q_, k_, v_: (q_ * scale, k_, v_), [BF16] * 3, name="fx_qkv")
    c = _fox_gate(x, g, _pad_cols(w_qkvf[:, 3 * D:], LANES).astype(BF16), _pad_cols(_row(b_f), LANES))
    tk = _tile(T, tk)
    ct = jnp.transpose(c[:, :, :H], (0, 2, 1)).reshape(B, H, T // tk, tk)
    o = _fox_attn(q.reshape(B, T, D), k.reshape(B, T, D), v.reshape(B, T, D), c, ct, H, tq=tk, tk=tk)
    out = _mm_res([o.reshape(B * T, D)], w_o.astype(BF16), x.reshape(B * T, D), lambda a: a, tn=D,
                  name="fx_out")
    return out.reshape(B, T, D)


def kernel(x, norm1_g, norm2_g, mlp_w1, mlp_w2, cc_w_in, cc_b_in, cc_dw, cc_dw_b, cc_ln_g, cc_ln_b, cc_w_out, cc_b_out, rw_mu, rw_w_rkv, rw_w0, rw_w1, rw_w2, rw_a0, rw_a1, rw_a2, rw_g1, rw_g2, rw_k_k, rw_k_a, rw_r_k, rw_gn_g, rw_gn_b, rw_w_o, sc_w_in, sc_conv_w, sc_w_out, fx_w_qkvf, fx_b_f, fx_w_o, final_g):
    B, T, D = x.shape
    depth = norm1_g.shape[0]
    n_mixers = 4
    for i in range(depth):
        m, j = i % n_mixers, i // n_mixers
        g = norm1_g[i]
        if m == 0:
            x = _conformer(x, g, cc_w_in[j], cc_b_in[j], cc_dw[j], cc_dw_b[j], cc_ln_g[j], cc_ln_b[j],
                           cc_w_out[j], cc_b_out[j])
        elif m == 1:
            x = _rwkv7(x, g, rw_mu[j], rw_w_rkv[j], rw_w0[j], rw_w1[j], rw_w2[j], rw_a0[j], rw_a1[j], rw_a2[j],
                       rw_g1[j], rw_g2[j], rw_k_k[j], rw_k_a[j], rw_r_k[j], rw_gn_g[j], rw_gn_b[j], rw_w_o[j])
        elif m == 2:
            x = _short_conv(x, g, sc_w_in[j], sc_conv_w[j], sc_w_out[j])
        else:
            x = _fox(x, g, fx_w_qkvf[j], fx_b_f[j], fx_w_o[j])
        x = _mlp(x.reshape(B * T, D), norm2_g[i], mlp_w1[i].astype(BF16), mlp_w2[i].astype(BF16),
                 final_g=final_g if i == depth - 1 else None).reshape(B, T, D)
    return x
```
